```python
import jax, jax.numpy as jnp
from jax import lax
import numpy as np

D_MODEL = 2048
BATCH = 8
SEQ = 2048
DEPTH = 1
DEC_BATCH = 128
DEC_SEQ = 4
PAST_LEN = 16384
PAGE_SIZE = 128

MLA_HEADS = 8
MLA_Q_RANK = 512
MLA_KV_RANK = 512
MLA_NOPE = 128
MLA_ROPE = 64
MLA_V = 128
MLA_QH = MLA_NOPE + MLA_ROPE
RET_HEADS = 8
RET_DK = 128
RET_DV = 256
RET_CHUNK = 128
MEM_TOKENS = 256
MEM_HEADS = 4
MEM_HD = 256
N_BRANCH = 3
N_GROUPS = 4
EXPERTS_PER_GROUP = 8
N_EXPERTS = N_GROUPS * EXPERTS_PER_GROUP
TOP_K = 2
D_EXPERT = 512
MOE_BLOCK = 128
Q_BLOCK = 128
ROPE_THETA = 10000.0
NORM_EPS = 1e-6
GN_EPS = 1e-5

MLA_WIDTH = MLA_HEADS * MLA_V
RET_QK_WIDTH = RET_HEADS * RET_DK
RET_V_WIDTH = RET_HEADS * RET_DV
MEM_WIDTH = MEM_HEADS * MEM_HD
IN_SPLITS = (MLA_Q_RANK, MLA_KV_RANK, MLA_ROPE, RET_QK_WIDTH, RET_QK_WIDTH, RET_V_WIDTH, RET_V_WIDTH, MEM_WIDTH, N_BRANCH * D_MODEL)
IN_WIDTH = sum(IN_SPLITS)

kernel_name = 'hybrid_mla_retention_hmoe_step'


def rms_norm(x, w):
    xf = x.astype(jnp.float32)
    y = xf * lax.rsqrt(jnp.mean(xf * xf, axis=-1, keepdims=True) + NORM_EPS)
    return (y * w.astype(jnp.float32)).astype(x.dtype)


def rope(x, pos):
    half = x.shape[-1] // 2
    inv = ROPE_THETA ** (-jnp.arange(half, dtype=jnp.float32) / half)
    ang = pos.astype(jnp.float32)[:, None] * inv[None, :]
    cos = jnp.cos(ang)[:, None, :]
    sin = jnp.sin(ang)[:, None, :]
    xf = x.astype(jnp.float32)
    x1, x2 = xf[..., :half], xf[..., half:]
    return jnp.concatenate([x1 * cos - x2 * sin, x2 * cos + x1 * sin], axis=-1).astype(x.dtype)


def retention_log_decay():
    return jnp.log1p(-jnp.exp2(-5.0 - jnp.arange(RET_HEADS, dtype=jnp.float32)))


def project_inputs(h, pos, w_in, q_norm_w, w_uq, kv_norm_w):
    B, S, _ = h.shape
    z = jnp.einsum('bsd,de->bse', h, w_in)
    idx = np.cumsum(np.array(IN_SPLITS))[:-1].tolist()
    cq, ckv, kr, rq, rk, rv, rg, mq, gz = jnp.split(z, idx, axis=-1)
    q = jnp.einsum('bsr,rhd->bshd', rms_norm(cq, q_norm_w), w_uq)
    q_nope = q[..., :MLA_NOPE]
    q_rope = rope(q[..., MLA_NOPE:], pos)
    c = rms_norm(ckv, kv_norm_w)
    k_rope = rope(kr[:, :, None, :], pos)[:, :, 0, :]
    rq = rope(rq.reshape(B, S, RET_HEADS, RET_DK), pos)
    rk = rope(rk.reshape(B, S, RET_HEADS, RET_DK), pos) * (RET_DK ** -0.5)
    rv = rv.reshape(B, S, RET_HEADS, RET_DV)
    rg = rg.reshape(B, S, RET_HEADS, RET_DV)
    mq = mq.reshape(B, S, MEM_HEADS, MEM_HD)
    gates = jax.nn.sigmoid(gz.astype(jnp.float32)).astype(h.dtype).reshape(B, S, N_BRANCH, D_MODEL)
    return q_nope, q_rope, c, k_rope, rq, rk, rv, rg, mq, gates


def mla_prompt(q_nope, q_rope, c, k_rope, w_uk, w_uv):
    B, S, H, _ = q_nope.shape
    k_nope = jnp.einsum('bsr,rhd->bshd', c, w_uk)
    v = jnp.einsum('bsr,rhd->bshd', c, w_uv)
    nb = S // Q_BLOCK
    qn_b = q_nope.reshape(B, nb, Q_BLOCK, H, MLA_NOPE).transpose(1, 0, 2, 3, 4)
    qr_b = q_rope.reshape(B, nb, Q_BLOCK, H, MLA_ROPE).transpose(1, 0, 2, 3, 4)
    scale = MLA_QH ** -0.5
    kpos = jnp.arange(S)

    def block(args):
        qn, qr, i = args
        s = jnp.einsum('bqhd,bkhd->bhqk', qn, k_nope) + jnp.einsum('bqhd,bkd->bhqk', qr, k_rope)
        s = s.astype(jnp.float32) * scale
        qpos = i * Q_BLOCK + jnp.arange(Q_BLOCK)
        s = jnp.where(kpos[None, :] <= qpos[:, None], s, -jnp.inf)
        p = jax.nn.softmax(s, axis=-1).astype(v.dtype)
        return jnp.einsum('bhqk,bkhd->bqhd', p, v)

    o = lax.map(block, (qn_b, qr_b, jnp.arange(nb)))
    return o.transpose(1, 0, 2, 3, 4).reshape(B, S, H * MLA_V)


def mla_sample(q_nope, q_rope, c_new, kr_new, cache_ckv, cache_krope, layer, page_table, w_uk, w_uv):
    B, T, H, _ = q_nope.shape
    q_lat = jnp.einsum('bthd,rhd->bthr', q_nope, w_uk)
    scale = MLA_QH ** -0.5
    causal = jnp.arange(T)[None, :] <= jnp.arange(T)[:, None]

    def one(args):
        pages, ql, qr, cn, krn = args
        c_past = cache_ckv[layer, pages].reshape(-1, MLA_KV_RANK)
        kr_past = cache_krope[layer, pages].reshape(-1, MLA_ROPE)
        P = c_past.shape[0]
        s_past = jnp.einsum('thr,pr->htp', ql, c_past) + jnp.einsum('thd,pd->htp', qr, kr_past)
        s_new = jnp.einsum('thr,pr->htp', ql, cn) + jnp.einsum('thd,pd->htp', qr, krn)
        s_new = jnp.where(causal[None], s_new.astype(jnp.float32), -jnp.inf)
        s = jnp.concatenate([s_past.astype(jnp.float32), s_new], axis=-1) * scale
        p = jax.nn.softmax(s, axis=-1).astype(c_past.dtype)
        return jnp.einsum('htp,pr->thr', p[..., :P], c_past) + jnp.einsum('htp,pr->thr', p[..., P:], cn)

    o_lat = lax.map(one, (page_table, q_lat, q_rope, c_new, kr_new))
    return jnp.einsum('bthr,rhd->bthd', o_lat, w_uv).reshape(B, T, H * MLA_V)


def retention_chunk(q, k, v, state, log_gamma):
    C = q.shape[1]
    i = jnp.arange(C, dtype=jnp.float32)
    diff = i[:, None] - i[None, :]
    decay = jnp.where(diff >= 0, jnp.exp(jnp.maximum(diff, 0.0)[None] * log_gamma[:, None, None]), 0.0)
    qf, kf, vf = q.astype(jnp.float32), k.astype(jnp.float32), v.astype(jnp.float32)
    s = jnp.einsum('bihd,bjhd->bhij', qf, kf) * decay[None]
    o_in = jnp.einsum('bhij,bjhv->bihv', s, vf)
    xi = jnp.exp((i + 1.0)[None, :] * log_gamma[:, None])
    o_cross = jnp.einsum('bihd,bhdv->bihv', qf, state) * xi.T[None, :, :, None]
    to_end = jnp.exp((C - 1.0 - i)[None, :] * log_gamma[:, None])
    new_state = jnp.exp(C * log_gamma)[None, :, None, None] * state + jnp.einsum('bjhd,bjhv,hj->bhdv', kf, vf, to_end)
    return o_in + o_cross, new_state


def retention_prompt(rq, rk, rv, log_gamma):
    B, S, H, _ = rq.shape
    nc = S // RET_CHUNK
    to_chunks = lambda a: a.reshape(B, nc, RET_CHUNK, H, a.shape[-1]).transpose(1, 0, 2, 3, 4)

    def step(st, xs):
        qc, kc, vc = xs
        o, st_new = retention_chunk(qc, kc, vc, st, log_gamma)
        return st_new, o

    st0 = jnp.zeros((B, H, RET_DK, RET_DV), jnp.float32)
    st, o = lax.scan(step, st0, (to_chunks(rq), to_chunks(rk), to_chunks(rv)))
    return o.transpose(1, 0, 2, 3, 4).reshape(B, S, H, RET_DV), st


def retention_out(o, rg, gn_w):
    B, S = o.shape[:2]
    mu = jnp.mean(o, axis=-1, keepdims=True)
    var = jnp.mean(jnp.square(o - mu), axis=-1, keepdims=True)
    on = (o - mu) * lax.rsqrt(var + GN_EPS) * gn_w.astype(jnp.float32)
    out = on * jax.nn.silu(rg.astype(jnp.float32))
    return out.reshape(B, S, RET_V_WIDTH).astype(rg.dtype)


def memory_kv(mem, norm_w, w_kv):
    B, M, _ = mem.shape
    kv = jnp.einsum('bmd,de->bme', rms_norm(mem, norm_w), w_kv)
    k = kv[..., :MEM_WIDTH].reshape(B, M, MEM_HEADS, MEM_HD)
    v = kv[..., MEM_WIDTH:].reshape(B, M, MEM_HEADS, MEM_HD)
    return k, v


def memory_attn(mq, mk, mv):
    B, S = mq.shape[:2]
    s = jnp.einsum('bshd,bmhd->bhsm', mq, mk).astype(jnp.float32) * (MEM_HD ** -0.5)
    p = jax.nn.softmax(s, axis=-1).astype(mv.dtype)
    return jnp.einsum('bhsm,bmhd->bshd', p, mv).reshape(B, S, MEM_WIDTH)


def merge_branches(gates, o_a, o_b, o_c, w_ba, w_bb, w_bc, w_o):
    merged = gates[:, :, 0] * (o_a @ w_ba) + gates[:, :, 1] * (o_b @ w_bb) + gates[:, :, 2] * (o_c @ w_bc)
    return merged @ w_o


def hier_moe(h, w_grp, b_grp, w_exp, b_exp, w_gate, w_up, w_down):
    lead = h.shape[:-1]
    x = h.reshape(-1, D_MODEL)
    T = x.shape[0]
    grp_p = jax.nn.softmax((x @ w_grp + b_grp).astype(jnp.float32), axis=-1)
    g_prob, g_idx = lax.top_k(grp_p, 1)
    e_logit = (x @ w_exp + b_exp).astype(jnp.float32).reshape(T, N_GROUPS, EXPERTS_PER_GROUP)
    in_grp = jnp.take_along_axis(e_logit, g_idx[:, :, None], axis=1)[:, 0, :]
    top_v, top_i = lax.top_k(in_grp, TOP_K)
    gate = g_prob * jax.nn.softmax(top_v, axis=-1)
    expert = (g_idx * EXPERTS_PER_GROUP + top_i).reshape(-1).astype(jnp.int32)
    A = T * TOP_K
    tok = jnp.arange(A, dtype=jnp.int32) // TOP_K
    order = jnp.argsort(expert)
    e_sorted = expert[order]
    counts = jax.ops.segment_sum(jnp.ones((A,), jnp.int32), expert, num_segments=N_EXPERTS)
    padded = (counts + MOE_BLOCK - 1) // MOE_BLOCK * MOE_BLOCK
    pad_end = jnp.cumsum(padded)
    pad_start = pad_end - padded
    cnt_start = jnp.cumsum(counts) - counts
    rank = jnp.arange(A, dtype=jnp.int32) - cnt_start[e_sorted]
    dest = jnp.zeros((A,), jnp.int32).at[order].set((pad_start[e_sorted] + rank).astype(jnp.int32))
    n_blk = (A + N_EXPERTS * (MOE_BLOCK - 1) + MOE_BLOCK - 1) // MOE_BLOCK
    row_tok = jnp.zeros((n_blk * MOE_BLOCK,), jnp.int32).at[dest].set(tok)
    blk_start = jnp.arange(n_blk, dtype=jnp.int32) * MOE_BLOCK
    blk_expert = jnp.minimum(jnp.sum(pad_end[None, :] <= blk_start[:, None], axis=1), N_EXPERTS - 1)
    x_blk = x[row_tok].reshape(n_blk, MOE_BLOCK, D_MODEL)

    def expert_block(args):
        xb, e = args
        return (jax.nn.silu(xb @ w_gate[e]) * (xb @ w_up[e])) @ w_down[e]

    y_rows = lax.map(expert_block, (x_blk, blk_expert)).reshape(n_blk * MOE_BLOCK, D_MODEL)
    y = y_rows[dest].reshape(T, TOP_K, D_MODEL)
    out = jnp.einsum('tkd,tk->td', y, gate.astype(y.dtype))
    return out.reshape(*lead, D_MODEL)


def setup_inputs(seed: int = 0) -> dict:
    key = jax.random.key(seed)
    ks = jax.random.split(key, 40)
    f32 = jnp.float32

    def nrm(k, shape, scale=1.0):
        return jax.random.normal(k, shape, f32) * scale

    def gain(k, shape):
        return 1.0 + 0.02 * jax.random.normal(k, shape, f32)

    n_pages = PAST_LEN // PAGE_SIZE
    n_used = DEC_BATCH * n_pages
    n_pool = n_used + max(1, n_used // 4)
    page_table = jax.random.permutation(ks[9], n_pool)[:n_used].astype(jnp.int32).reshape(DEC_BATCH, n_pages)
    return {
        'x_prompt': nrm(ks[0], (BATCH, SEQ, D_MODEL)),
        'x_sample': nrm(ks[1], (DEC_BATCH, DEC_SEQ, D_MODEL)),
        'mem_prompt': nrm(ks[2], (BATCH, MEM_TOKENS, D_MODEL)),
        'cache_ckv': nrm(ks[3], (DEPTH, n_pool, PAGE_SIZE, MLA_KV_RANK)),
        'cache_krope': nrm(ks[4], (DEPTH, n_pool, PAGE_SIZE, MLA_ROPE)),
        'state_ret': nrm(ks[5], (DEPTH, DEC_BATCH, RET_HEADS, RET_DK, RET_DV)),
        'cache_mem_k': nrm(ks[6], (DEPTH, DEC_BATCH, MEM_TOKENS, MEM_HEADS, MEM_HD)),
        'cache_mem_v': nrm(ks[7], (DEPTH, DEC_BATCH, MEM_TOKENS, MEM_HEADS, MEM_HD)),
        'page_table': page_table,
        'attn_norm_w': gain(ks[10], (DEPTH, D_MODEL)),
        'w_in': nrm(ks[11], (DEPTH, D_MODEL, IN_WIDTH), D_MODEL ** -0.5),
        'mla_q_norm_w': gain(ks[12], (DEPTH, MLA_Q_RANK)),
        'mla_w_uq': nrm(ks[13], (DEPTH, MLA_Q_RANK, MLA_HEADS, MLA_QH), MLA_Q_RANK ** -0.5),
        'mla_kv_norm_w': gain(ks[14], (DEPTH, MLA_KV_RANK)),
        'mla_w_uk': nrm(ks[15], (DEPTH, MLA_KV_RANK, MLA_HEADS, MLA_NOPE), MLA_KV_RANK ** -0.5),
        'mla_w_uv': nrm(ks[16], (DEPTH, MLA_KV_RANK, MLA_HEADS, MLA_V), MLA_KV_RANK ** -0.5),
        'ret_gn_w': gain(ks[17], (DEPTH, RET_HEADS, RET_DV)),
        'mem_norm_w': gain(ks[18], (DEPTH, D_MODEL)),
        'mem_w_kv': nrm(ks[19], (DEPTH, D_MODEL, 2 * MEM_WIDTH), D_MODEL ** -0.5),
        'w_branch_a': nrm(ks[20], (DEPTH, MLA_WIDTH, D_MODEL), MLA_WIDTH ** -0.5),
        'w_branch_b': nrm(ks[21], (DEPTH, RET_V_WIDTH, D_MODEL), RET_V_WIDTH ** -0.5),
        'w_branch_c': nrm(ks[22], (DEPTH, MEM_WIDTH, D_MODEL), MEM_WIDTH ** -0.5),
        'w_out': nrm(ks[23], (DEPTH, D_MODEL, D_MODEL), D_MODEL ** -0.5),
        'ffn_norm_w': gain(ks[24], (DEPTH, D_MODEL)),
        'router_grp_w': nrm(ks[25], (DEPTH, D_MODEL, N_GROUPS), D_MODEL ** -0.5),
        'router_grp_b': nrm(ks[26], (DEPTH, N_GROUPS), 0.01),
        'router_exp_w': nrm(ks[27], (DEPTH, D_MODEL, N_EXPERTS), D_MODEL ** -0.5),
        'router_exp_b': nrm(ks[28], (DEPTH, N_EXPERTS), 0.01),
        'exp_w_gate': nrm(ks[29], (DEPTH, N_EXPERTS, D_MODEL, D_EXPERT), D_MODEL ** -0.5),
        'exp_w_up': nrm(ks[30], (DEPTH, N_EXPERTS, D_MODEL, D_EXPERT), D_MODEL ** -0.5),
        'exp_w_down': nrm(ks[31], (DEPTH, N_EXPERTS, D_EXPERT, D_MODEL), D_EXPERT ** -0.5),
        'final_norm_w': gain(ks[32], (D_MODEL,)),
    }


def reference(x_prompt, x_sample, mem_prompt, cache_ckv, cache_krope, state_ret, cache_mem_k, cache_mem_v,
              page_table, attn_norm_w, w_in, mla_q_norm_w, mla_w_uq, mla_kv_norm_w, mla_w_uk, mla_w_uv,
              ret_gn_w, mem_norm_w, mem_w_kv, w_branch_a, w_branch_b, w_branch_c, w_out, ffn_norm_w,
              router_grp_w, router_grp_b, router_exp_w, router_exp_b, exp_w_gate, exp_w_up, exp_w_down,
              final_norm_w):
    log_gamma = retention_log_decay()
    pos_p = jnp.arange(x_prompt.shape[1])
    pos_s = PAST_LEN + jnp.arange(x_sample.shape[1])
    xp, xs = x_prompt, x_sample
    p_ckv_l, p_kr_l, p_ret_l, p_mk_l, p_mv_l = [], [], [], [], []
    s_ckv_l, s_kr_l, s_ret_l = [], [], []
    for l in range(DEPTH):
        hp = rms_norm(xp, attn_norm_w[l])
        qn, qr, c, kr, rq, rk, rv, rg, mq, gates = project_inputs(hp, pos_p, w_in[l], mla_q_norm_w[l], mla_w_uq[l], mla_kv_norm_w[l])
        o_a = mla_prompt(qn, qr, c, kr, mla_w_uk[l], mla_w_uv[l])
        o_ret, st_p = retention_prompt(rq, rk, rv, log_gamma)
        o_b = retention_out(o_ret, rg, ret_gn_w[l])
        mk, mv = memory_kv(mem_prompt, mem_norm_w[l], mem_w_kv[l])
        o_c = memory_attn(mq, mk, mv)
        xp = xp + merge_branches(gates, o_a, o_b, o_c, w_branch_a[l], w_branch_b[l], w_branch_c[l], w_out[l])
        xp = xp + hier_moe(rms_norm(xp, ffn_norm_w[l]), router_grp_w[l], router_grp_b[l], router_exp_w[l], router_exp_b[l], exp_w_gate[l], exp_w_up[l], exp_w_down[l])
        p_ckv_l.append(c)
        p_kr_l.append(kr)
        p_ret_l.append(st_p.astype(x_prompt.dtype))
        p_mk_l.append(mk)
        p_mv_l.append(mv)
        hs = rms_norm(xs, attn_norm_w[l])
        qn, qr, c, kr, rq, rk, rv, rg, mq, gates = project_inputs(hs, pos_s, w_in[l], mla_q_norm_w[l], mla_w_uq[l], mla_kv_norm_w[l])
        o_a = mla_sample(qn, qr, c, kr, cache_ckv, cache_krope, l, page_table, mla_w_uk[l], mla_w_uv[l])
        o_ret, st_s = retention_chunk(rq, rk, rv, state_ret[l].astype(jnp.float32), log_gamma)
        o_b = retention_out(o_ret, rg, ret_gn_w[l])
        o_c = memory_attn(mq, cache_mem_k[l], cache_mem_v[l])
        xs = xs + merge_branches(gates, o_a, o_b, o_c, w_branch_a[l], w_branch_b[l], w_branch_c[l], w_out[l])
        xs = xs + hier_moe(rms_norm(xs, ffn_norm_w[l]), router_grp_w[l], router_grp_b[l], router_exp_w[l], router_exp_b[l], exp_w_gate[l], exp_w_up[l], exp_w_down[l])
        s_ckv_l.append(c)
        s_kr_l.append(kr)
        s_ret_l.append(st_s.astype(state_ret.dtype))
    y_prompt = rms_norm(xp, final_norm_w)
    y_sample = rms_norm(xs, final_norm_w)
    p_ckv = jnp.stack(p_ckv_l)
    p_krope = jnp.stack(p_kr_l)
    p_ret = jnp.stack(p_ret_l)
    p_mem_k = jnp.stack(p_mk_l)
    p_mem_v = jnp.stack(p_mv_l)
    s_ckv = jnp.stack(s_ckv_l)
    s_krope = jnp.stack(s_kr_l)
    s_ret = jnp.stack(s_ret_l)
    return (y_prompt, y_sample, p_ckv, p_krope, p_ret, p_mem_k, p_mem_v, s_ckv, s_krope, s_ret)
```

```python
import functools

import jax
import jax.numpy as jnp
from jax import lax
from jax.experimental import pallas as pl
from jax.experimental.pallas import tpu as pltpu

F32 = jnp.float32
BF16 = jnp.bfloat16

NORM_EPS = 1e-6
GN_EPS = 1e-5
ROPE_THETA = 10000.0
RET_CHUNK = 128
MOE_BLOCK = 128
LANES = 128
NEG_BIG = -1e30
VMEM_LIMIT_BYTES = 56 * 1024 * 1024


def _params(*sem):
    return pltpu.CompilerParams(dimension_semantics=sem, vmem_limit_bytes=VMEM_LIMIT_BYTES)


def _dot(a, b):
    return jnp.dot(a, b, preferred_element_type=F32)


def _dot_nt(a, b):
    return lax.dot_general(a, b, (((1,), (1,)), ((), ())), preferred_element_type=F32)


def _rms(x, w):
    return x * lax.rsqrt(jnp.mean(x * x, axis=-1, keepdims=True) + NORM_EPS) * w


def _rope64(x, cos, sin):
    lane = lax.broadcasted_iota(jnp.int32, x.shape, 1)
    first = (lane & 63) < 32
    swapped = jnp.where(first, pltpu.roll(x, 96, 1), pltpu.roll(x, 32, 1))
    return x * cos + swapped * sin


def _rope128(x, cos, sin):
    return x * cos + pltpu.roll(x, 64, 1) * sin


def _rope_tables(pos, dim):
    half = dim // 2
    inv = ROPE_THETA ** (-jnp.arange(half, dtype=F32) / half)
    ang = pos.astype(F32)[:, None] * inv[None, :]
    cos = jnp.cos(ang)
    sin = jnp.sin(ang)
    c = jnp.concatenate([cos, cos], axis=-1)
    s = jnp.concatenate([-sin, sin], axis=-1)
    reps = LANES // dim
    return jnp.tile(c, (1, reps)), jnp.tile(s, (1, reps))


def _norm_proj_kernel(x_ref, nw_ref, w_ref, z_ref, h_ref):
    @pl.when(pl.program_id(1) == 0)
    def _():
        h_ref[...] = _rms(x_ref[...], nw_ref[...]).astype(BF16)

    z_ref[...] = _dot(h_ref[...], w_ref[...])


def _norm_proj_extra_kernel(x_ref, nw_ref, w_ref, we_ref, z_ref, ze_ref, h_ref):
    @pl.when(pl.program_id(1) == 0)
    def _():
        h = _rms(x_ref[...], nw_ref[...]).astype(BF16)
        h_ref[...] = h
        ze_ref[...] = _dot(h, we_ref[...])

    z_ref[...] = _dot(h_ref[...], w_ref[...])


def _norm_proj(x, norm_w, w, w_extra=None, *, tm, tn):
    t, d = x.shape
    n = w.shape[1]
    grid = (t // tm, n // tn)
    in_specs = [
        pl.BlockSpec((tm, d), lambda i, j: (i, 0)),
        pl.BlockSpec((1, d), lambda i, j: (0, 0)),
        pl.BlockSpec((d, tn), lambda i, j: (0, j)),
    ]
    out_specs = [pl.BlockSpec((tm, tn), lambda i, j: (i, j))]
    out_shape = [jax.ShapeDtypeStruct((t, n), F32)]
    args = [x, norm_w.reshape(1, d), w]
    body = _norm_proj_kernel
    if w_extra is not None:
        ne = w_extra.shape[1]
        in_specs.append(pl.BlockSpec((d, ne), lambda i, j: (0, 0)))
        out_specs.append(pl.BlockSpec((tm, ne), lambda i, j: (i, 0)))
        out_shape.append(jax.ShapeDtypeStruct((t, ne), F32))
        args.append(w_extra)
        body = _norm_proj_extra_kernel
    return pl.pallas_call(
        body,
        grid=grid,
        in_specs=in_specs,
        out_specs=out_specs,
        out_shape=out_shape,
        scratch_shapes=[pltpu.VMEM((tm, d), BF16)],
        compiler_params=_params("parallel", "arbitrary"),
    )(*args)


def _mla_prep_kernel(cq_ref, ckv_ref, zk_ref, cos_ref, sin_ref, qnw_ref, kvnw_ref, wqn_ref, wqr_ref,
                     wuk_ref, wuv_ref, qcat_ref, c_ref, kr_ref, kcat_ref, v_ref, *, n_heads, rope_dim):
    cos = cos_ref[...]
    sin = sin_ref[...]
    cqn = _rms(cq_ref[...], qnw_ref[...]).astype(BF16)
    qn = _dot(cqn, wqn_ref[...])
    qr = _dot(cqn, wqr_ref[...])
    c = _rms(ckv_ref[...], kvnw_ref[...])
    c_ref[...] = c
    cb = c.astype(BF16)
    kn = _dot(cb, wuk_ref[...])
    v_ref[...] = _dot(cb, wuv_ref[...]).astype(BF16)
    kr = _rope64(zk_ref[...], cos, sin)
    kr_ref[...] = kr[:, :rope_dim]
    krb = kr.astype(BF16)
    for h in range(n_heads):
        lo = h * 2 * LANES
        sl = slice(h * LANES, (h + 1) * LANES)
        qcat_ref[:, lo:lo + LANES] = qn[:, sl].astype(BF16)
        qcat_ref[:, lo + LANES:lo + 2 * LANES] = _rope64(qr[:, sl], cos, sin).astype(BF16)
        kcat_ref[:, lo:lo + LANES] = kn[:, sl].astype(BF16)
        kcat_ref[:, lo + LANES:lo + 2 * LANES] = krb


def _mla_prep(z, zk, cos64, sin64, qnw, kvnw, wqn, wqr, wuk, wuv, *, tm, n_heads, q_rank, kv_rank, rope_dim):
    t = z.shape[0]
    row = lambda i: (i, 0)
    const = lambda i: (0, 0)
    hw = n_heads * LANES
    return pl.pallas_call(
        functools.partial(_mla_prep_kernel, n_heads=n_heads, rope_dim=rope_dim),
        grid=(t // tm,),
        in_specs=[
            pl.BlockSpec((tm, q_rank), lambda i: (i, 0)),
            pl.BlockSpec((tm, kv_rank), lambda i: (i, q_rank // kv_rank)),
            pl.BlockSpec((tm, LANES), row),
            pl.BlockSpec((tm, LANES), row),
            pl.BlockSpec((tm, LANES), row),
            pl.BlockSpec((1, q_rank), const),
            pl.BlockSpec((1, kv_rank), const),
            pl.BlockSpec((q_rank, hw), const),
            pl.BlockSpec((q_rank, hw), const),
            pl.BlockSpec((kv_rank, hw), const),
            pl.BlockSpec((kv_rank, hw), const),
        ],
        out_specs=[
            pl.BlockSpec((tm, 2 * hw), row),
            pl.BlockSpec((tm, kv_rank), row),
            pl.BlockSpec((tm, rope_dim), row),
            pl.BlockSpec((tm, 2 * hw), row),
            pl.BlockSpec((tm, hw), row),
        ],
        out_shape=[
            jax.ShapeDtypeStruct((t, 2 * hw), BF16),
            jax.ShapeDtypeStruct((t, kv_rank), F32),
            jax.ShapeDtypeStruct((t, rope_dim), F32),
            jax.ShapeDtypeStruct((t, 2 * hw), BF16),
            jax.ShapeDtypeStruct((t, hw), BF16),
        ],
        compiler_params=_params("parallel"),
    )(z, z, zk, cos64, sin64, qnw, kvnw, wqn, wqr, wuk, wuv)


def _flash_kernel(q_ref, k_ref, v_ref, o_ref, *, tq, scale):
    qi = pl.program_id(2)
    q = q_ref[...]
    rows = qi * tq + lax.broadcasted_iota(jnp.int32, (tq, tq), 0)
    cols = lax.broadcasted_iota(jnp.int32, (tq, tq), 1)

    def body(j, carry):
        m, l, acc = carry
        start = pl.multiple_of(j * tq, tq)
        k = k_ref[pl.ds(start, tq), :]
        v = v_ref[pl.ds(start, tq), :]
        s = _dot_nt(q, k) * scale
        s = jnp.where(cols + j * tq <= rows, s, NEG_BIG)
        m_new = jnp.maximum(m, jnp.max(s, axis=-1, keepdims=True))
        alpha = jnp.exp(m - m_new)
        p = jnp.exp(s - m_new)
        l = alpha * l + jnp.sum(p, axis=-1, keepdims=True)
        acc = alpha * acc + _dot(p.astype(BF16), v)
        return m_new, l, acc

    dv = v_ref.shape[-1]
    init = (jnp.full((tq, 1), NEG_BIG, F32), jnp.zeros((tq, 1), F32), jnp.zeros((tq, dv), F32))
    _, l, acc = lax.fori_loop(0, qi + 1, body, init)
    o_ref[...] = (acc / l).astype(BF16)


def _mla_prompt(qcat, kcat, v, *, batch, seq, n_heads, tq, scale):
    nq = seq // tq
    dv = v.shape[1] // n_heads
    return pl.pallas_call(
        functools.partial(_flash_kernel, tq=tq, scale=scale),
        grid=(batch, n_heads, nq),
        in_specs=[
            pl.BlockSpec((tq, 2 * LANES), lambda b, h, i: (b * nq + i, h)),
            pl.BlockSpec((seq, 2 * LANES), lambda b, h, i: (b, h)),
            pl.BlockSpec((seq, dv), lambda b, h, i: (b, h)),
        ],
        out_specs=pl.BlockSpec((tq, dv), lambda b, h, i: (b * nq + i, h)),
        out_shape=jax.ShapeDtypeStruct((batch * seq, n_heads * dv), BF16),
        compiler_params=_params("parallel", "parallel", "arbitrary"),
    )(qcat, kcat, v)


def _absorb_q_kernel(qcat_ref, wukt_ref, qlat_ref, qr_ref, *, n_heads, kv_rank):
    for h in range(n_heads):
        lo = h * 2 * LANES
        qlat_ref[:, h * kv_rank:(h + 1) * kv_rank] = _dot(qcat_ref[:, lo:lo + LANES], wukt_ref[h]).astype(BF16)
        qr_ref[:, h * LANES:(h + 1) * LANES] = qcat_ref[:, lo + LANES:lo + 2 * LANES]


def _absorb_q(qcat, wukt, *, ts, row_block, n_heads, kv_rank):
    return pl.pallas_call(
        functools.partial(_absorb_q_kernel, n_heads=n_heads, kv_rank=kv_rank),
        grid=(1,),
        in_specs=[
            pl.BlockSpec((ts, qcat.shape[1]), lambda i: (row_block, 0)),
            pl.BlockSpec(wukt.shape, lambda i: (0, 0, 0)),
        ],
        out_specs=[
            pl.BlockSpec((ts, n_heads * kv_rank), lambda i: (0, 0)),
            pl.BlockSpec((ts, n_heads * LANES), lambda i: (0, 0)),
        ],
        out_shape=[
            jax.ShapeDtypeStruct((ts, n_heads * kv_rank), BF16),
            jax.ShapeDtypeStruct((ts, n_heads * LANES), BF16),
        ],
        compiler_params=_params("arbitrary"),
    )(qcat, wukt)


def _decode_kernel(pt_ref, qlat_ref, qr_ref, cnew_ref, krnew_ref, *rest, npg, scale, n_heads, rope_dim):
    ckv_refs = rest[:npg]
    kr_refs = rest[npg:2 * npg]
    o_ref = rest[2 * npg]
    m_ref, l_ref, acc_ref = rest[2 * npg + 1:]
    step = pl.program_id(1)

    @pl.when(step == 0)
    def _():
        m_ref[...] = jnp.full(m_ref.shape, NEG_BIG, F32)
        l_ref[...] = jnp.zeros(l_ref.shape, F32)
        acc_ref[...] = jnp.zeros(acc_ref.shape, F32)

    ql = qlat_ref[0]
    qr = qr_ref[0][:, :rope_dim]

    def update(s, vals):
        m_prev = m_ref[...]
        m_new = jnp.maximum(m_prev, jnp.max(s, axis=-1, keepdims=True))
        alpha = jnp.exp(m_prev - m_new)
        p = jnp.exp(s - m_new)
        l_ref[...] = alpha * l_ref[...] + jnp.sum(p, axis=-1, keepdims=True)
        pb = p.astype(BF16)
        acc = alpha * acc_ref[...]
        width = vals[0].shape[0]
        for g, val in enumerate(vals):
            acc = acc + _dot(pb[:, g * width:(g + 1) * width], val)
        acc_ref[...] = acc
        m_ref[...] = m_new

    vals = []
    scores = []
    for g in range(npg):
        cb = ckv_refs[g][...].astype(BF16)
        kb = kr_refs[g][...].astype(BF16)
        scores.append((_dot_nt(ql, cb) + _dot_nt(qr, kb)) * scale)
        vals.append(cb)
    update(jnp.concatenate(scores, axis=-1), vals)

    @pl.when(step == pl.num_programs(1) - 1)
    def _():
        cn = cnew_ref[0]
        krn = krnew_ref[0]
        s = (_dot_nt(ql, cn) + _dot_nt(qr, krn)) * scale
        t_row = lax.broadcasted_iota(jnp.int32, s.shape, 0) >> (n_heads.bit_length() - 1)
        col = lax.broadcasted_iota(jnp.int32, s.shape, 1)
        update(jnp.where(col <= t_row, s, NEG_BIG), [cn])
        o_ref[0] = (acc_ref[...] / l_ref[...]).astype(BF16)


def _mla_decode(page_table, qlat, qr, cnew, krnew, cache_ckv, cache_krope, *, npg, scale, n_heads, rope_dim):
    nb, rows, kv_rank = qlat.shape
    n_pages = page_table.shape[1]
    page = cache_ckv.shape[1]
    steps = n_pages // npg
    ckv_specs = [
        pl.BlockSpec((None, page, kv_rank), functools.partial(lambda b, p, pt, g: (pt[b, p * npg + g], 0, 0), g=g))
        for g in range(npg)
    ]
    kr_specs = [
        pl.BlockSpec((None, page, rope_dim), functools.partial(lambda b, p, pt, g: (pt[b, p * npg + g], 0, 0), g=g))
        for g in range(npg)
    ]
    per_b = lambda b, p, pt: (b, 0, 0)
    grid_spec = pltpu.PrefetchScalarGridSpec(
        num_scalar_prefetch=1,
        grid=(nb, steps),
        in_specs=[
            pl.BlockSpec((1, rows, kv_rank), per_b),
            pl.BlockSpec((1, rows, LANES), per_b),
            pl.BlockSpec((1,) + cnew.shape[1:], per_b),
            pl.BlockSpec((1,) + krnew.shape[1:], per_b),
        ] + ckv_specs + kr_specs,
        out_specs=pl.BlockSpec((1, rows, kv_rank), per_b),
        scratch_shapes=[
            pltpu.VMEM((rows, 1), F32),
            pltpu.VMEM((rows, 1), F32),
            pltpu.VMEM((rows, kv_rank), F32),
        ],
    )
    return pl.pallas_call(
        functools.partial(_decode_kernel, npg=npg, scale=scale, n_heads=n_heads, rope_dim=rope_dim),
        grid_spec=grid_spec,
        out_shape=jax.ShapeDtypeStruct((nb, rows, kv_rank), BF16),
        compiler_params=_params("parallel", "arbitrary"),
    )(page_table, qlat, qr, cnew, krnew, *([cache_ckv] * npg), *([cache_krope] * npg))


def _head_proj_kernel(x_ref, w_ref, o_ref, *, n_heads):
    k = x_ref.shape[1] // n_heads
    n = o_ref.shape[1] // n_heads
    for h in range(n_heads):
        o_ref[:, h * n:(h + 1) * n] = _dot(x_ref[:, h * k:(h + 1) * k], w_ref[h]).astype(BF16)


def _head_proj(x, w, *, n_heads):
    rows = x.shape[0]
    n = w.shape[2]
    return pl.pallas_call(
        functools.partial(_head_proj_kernel, n_heads=n_heads),
        grid=(1,),
        in_specs=[pl.BlockSpec(x.shape, lambda i: (0, 0)), pl.BlockSpec(w.shape, lambda i: (0, 0, 0))],
        out_specs=pl.BlockSpec((rows, n_heads * n), lambda i: (0, 0)),
        out_shape=jax.ShapeDtypeStruct((rows, n_heads * n), BF16),
        compiler_params=_params("arbitrary"),
    )(x, w)


def _group_norm_gate(o, rg, gn_w):
    mu = jnp.mean(o, axis=-1, keepdims=True)
    d = o - mu
    var = jnp.mean(d * d, axis=-1, keepdims=True)
    on = d * lax.rsqrt(var + GN_EPS) * gn_w
    return (on * (rg * jax.nn.sigmoid(rg))).astype(BF16)


def _ret_prompt_kernel(q_ref, k_ref, v_ref, g_ref, cos_ref, sin_ref, lg_ref, gn_ref, o_ref, st_ref, state,
                       *, n_sub, chunk, k_scale):
    ci = pl.program_id(2)

    @pl.when(ci == 0)
    def _():
        state[...] = jnp.zeros(state.shape, F32)

    lg = lg_ref[0][:, :1]
    ii = lax.broadcasted_iota(jnp.int32, (chunk, chunk), 0)
    jj = lax.broadcasted_iota(jnp.int32, (chunk, chunk), 1)
    diff = (ii - jj).astype(F32)
    decay = jnp.where(diff >= 0, jnp.exp(jnp.maximum(diff, 0.0) * lg), 0.0)
    ri = lax.broadcasted_iota(jnp.int32, (chunk, 1), 0).astype(F32)
    xi = jnp.exp((ri + 1.0) * lg)
    to_end = jnp.exp((chunk - 1.0 - ri) * lg)
    g_chunk = jnp.exp(chunk * lg)
    gn_w = gn_ref[...]
    for sub in range(n_sub):
        sl = pl.ds(sub * chunk, chunk)
        cos = cos_ref[sl, :]
        sin = sin_ref[sl, :]
        q = _rope128(q_ref[sl, :], cos, sin)
        k = _rope128(k_ref[sl, :], cos, sin) * k_scale
        qb = q.astype(BF16)
        vb = v_ref[sl, :].astype(BF16)
        s = _dot_nt(qb, k.astype(BF16)) * decay
        st = state[...]
        o = _dot(s.astype(BF16), vb) + _dot(qb, st.astype(BF16)) * xi
        kw_t = jnp.transpose(k * to_end).astype(BF16)
        state[...] = g_chunk * st + _dot(kw_t, vb)
        o_ref[sl, :] = _group_norm_gate(o, g_ref[sl, :], gn_w)

    @pl.when(ci == pl.num_programs(2) - 1)
    def _():
        st_ref[0, 0] = state[...]


def _ret_prompt(z, cos128, sin128, lgam, gn_w, *, batch, seq, n_heads, dk, dv, rows, offs):
    rq_off, rk_off, rv_off, rg_off = offs
    nr = seq // rows
    qk_spec = lambda off: pl.BlockSpec((rows, dk), lambda b, h, c: (b * nr + c, off // dk + h))
    v_spec = lambda off: pl.BlockSpec((rows, dv), lambda b, h, c: (b * nr + c, off // dv + h))
    tab = pl.BlockSpec((rows, LANES), lambda b, h, c: (b * nr + c, 0))
    return pl.pallas_call(
        functools.partial(_ret_prompt_kernel, n_sub=rows // RET_CHUNK, chunk=RET_CHUNK, k_scale=dk ** -0.5),
        grid=(batch, n_heads, nr),
        in_specs=[
            qk_spec(rq_off), qk_spec(rk_off), v_spec(rv_off), v_spec(rg_off), tab, tab,
            pl.BlockSpec((1, 1, LANES), lambda b, h, c: (h, 0, 0)),
            pl.BlockSpec((1, dv), lambda b, h, c: (0, h)),
        ],
        out_specs=[
            pl.BlockSpec((rows, dv), lambda b, h, c: (b * nr + c, h)),
            pl.BlockSpec((1, 1, dk, dv), lambda b, h, c: (b, h, 0, 0)),
        ],
        out_shape=[
            jax.ShapeDtypeStruct((batch * seq, n_heads * dv), BF16),
            jax.ShapeDtypeStruct((batch, n_heads, dk, dv), F32),
        ],
        scratch_shapes=[pltpu.VMEM((dk, dv), F32)],
        compiler_params=_params("parallel", "parallel", "arbitrary"),
    )(z, z, z, z, cos128, sin128, lgam, gn_w)


def _ret_sample_kernel(q_ref, k_ref, v_ref, g_ref, cos_ref, sin_ref, lg_ref, gn_ref, st_in_ref, o_ref, st_out_ref,
                       *, nb, t_log2, k_scale):
    ts = 1 << t_log2
    rows = nb * ts
    lg = lg_ref[0][:, :1]
    ii = lax.broadcasted_iota(jnp.int32, (rows, rows), 0)
    jj = lax.broadcasted_iota(jnp.int32, (rows, rows), 1)
    diff = (ii - jj).astype(F32)
    keep = ((ii >> t_log2) == (jj >> t_log2)) & (ii >= jj)
    decay = jnp.where(keep, jnp.exp(jnp.maximum(diff, 0.0) * lg), 0.0)
    ri = lax.broadcasted_iota(jnp.int32, (rows, 1), 0)
    rb = ri >> t_log2
    ti = (ri & (ts - 1)).astype(F32)
    xi = jnp.exp((ti + 1.0) * lg)
    to_end = jnp.exp((ts - 1.0 - ti) * lg)
    g_chunk = jnp.exp(ts * lg)
    cos = cos_ref[...]
    sin = sin_ref[...]
    q = _rope128(q_ref[...], cos, sin)
    k = _rope128(k_ref[...], cos, sin) * k_scale
    qb = q.astype(BF16)
    vb = v_ref[...].astype(BF16)
    s = _dot_nt(qb, k.astype(BF16)) * decay
    o_in = _dot(s.astype(BF16), vb)
    kw = k * to_end
    o_cross = jnp.zeros(o_in.shape, F32)
    for b in range(nb):
        st = st_in_ref[b, 0]
        mine = rb == b
        o_cross = jnp.where(mine, _dot(qb, st.astype(BF16)), o_cross)
        kw_t = jnp.transpose(jnp.where(mine, kw, 0.0)).astype(BF16)
        st_out_ref[b, 0] = g_chunk * st + _dot(kw_t, vb)
    o_ref[...] = _group_norm_gate(o_in + o_cross * xi, g_ref[...], gn_ref[...])


def _ret_sample(z, cos128, sin128, lgam, gn_w, state, *, row0, dec_batch, dec_seq, n_heads, dk, dv, offs):
    rq_off, rk_off, rv_off, rg_off = offs
    t_log2 = dec_seq.bit_length() - 1
    assert 1 << t_log2 == dec_seq
    rows = LANES
    nb = rows // dec_seq
    nblk = dec_batch // nb
    rb0 = row0 // rows
    qk_spec = lambda off: pl.BlockSpec((rows, dk), lambda i, h: (rb0 + i, off // dk + h))
    v_spec = lambda off: pl.BlockSpec((rows, dv), lambda i, h: (rb0 + i, off // dv + h))
    tab = pl.BlockSpec((rows, LANES), lambda i, h: (rb0 + i, 0))
    return pl.pallas_call(
        functools.partial(_ret_sample_kernel, nb=nb, t_log2=t_log2, k_scale=dk ** -0.5),
        grid=(nblk, n_heads),
        in_specs=[
            qk_spec(rq_off), qk_spec(rk_off), v_spec(rv_off), v_spec(rg_off), tab, tab,
            pl.BlockSpec((1, 1, LANES), lambda i, h: (h, 0, 0)),
            pl.BlockSpec((1, dv), lambda i, h: (0, h)),
            pl.BlockSpec((nb, 1, dk, dv), lambda i, h: (i, h, 0, 0)),
        ],
        out_specs=[
            pl.BlockSpec((rows, dv), lambda i, h: (i, h)),
            pl.BlockSpec((nb, 1, dk, dv), lambda i, h: (i, h, 0, 0)),
        ],
        out_shape=[
            jax.ShapeDtypeStruct((dec_batch * dec_seq, n_heads * dv), BF16),
            jax.ShapeDtypeStruct(state.shape, F32),
        ],
        compiler_params=_params("parallel", "parallel"),
    )(z, z, z, z, cos128, sin128, lgam, gn_w, state)


def _softmax_rows(s):
    e = jnp.exp(s - jnp.max(s, axis=-1, keepdims=True))
    return e / jnp.sum(e, axis=-1, keepdims=True)


def _mem_attn_prompt_kernel(q_ref, k_ref, v_ref, o_ref, *, scale):
    s = _dot_nt(q_ref[...].astype(BF16), k_ref[...].astype(BF16)) * scale
    o_ref[...] = _dot(_softmax_rows(s).astype(BF16), v_ref[...].astype(BF16)).astype(BF16)


def _mem_attn_prompt(z, kv, *, batch, seq, mem_tokens, n_heads, hd, tq, mq_off):
    nq = seq // tq
    return pl.pallas_call(
        functools.partial(_mem_attn_prompt_kernel, scale=hd ** -0.5),
        grid=(batch, n_heads, nq),
        in_specs=[
            pl.BlockSpec((tq, hd), lambda b, h, i: (b * nq + i, mq_off // hd + h)),
            pl.BlockSpec((mem_tokens, hd), lambda b, h, i: (b, h)),
            pl.BlockSpec((mem_tokens, hd), lambda b, h, i: (b, n_heads + h)),
        ],
        out_specs=pl.BlockSpec((tq, hd), lambda b, h, i: (b * nq + i, h)),
        out_shape=jax.ShapeDtypeStruct((batch * seq, n_heads * hd), BF16),
        compiler_params=_params("parallel", "parallel", "parallel"),
    )(z, kv, kv)


def _mem_attn_sample_kernel(q_ref, k_ref, v_ref, o_ref, *, nb, t_log2, scale):
    q = q_ref[...].astype(BF16)
    rb = lax.broadcasted_iota(jnp.int32, (q.shape[0], 1), 0) >> t_log2
    o = jnp.zeros(o_ref.shape, F32)
    for b in range(nb):
        s = _dot_nt(q, k_ref[b].astype(BF16)) * scale
        ob = _dot(_softmax_rows(s).astype(BF16), v_ref[b].astype(BF16))
        o = jnp.where(rb == b, ob, o)
    o_ref[...] = o.astype(BF16)


def _mem_attn_sample(z, mem_k, mem_v, *, row0, dec_batch, dec_seq, n_heads, hd, nb, mq_off):
    t_log2 = dec_seq.bit_length() - 1
    rows = nb * dec_seq
    rb0 = row0 // rows
    mem_tokens = mem_k.shape[1]
    return pl.pallas_call(
        functools.partial(_mem_attn_sample_kernel, nb=nb, t_log2=t_log2, scale=hd ** -0.5),
        grid=(dec_batch // nb, n_heads),
        in_specs=[
            pl.BlockSpec((rows, hd), lambda i, h: (rb0 + i, mq_off // hd + h)),
            pl.BlockSpec((nb, mem_tokens, hd), lambda i, h: (i, 0, h)),
            pl.BlockSpec((nb, mem_tokens, hd), lambda i, h: (i, 0, h)),
        ],
        out_specs=pl.BlockSpec((rows, hd), lambda i, h: (i, h)),
        out_shape=jax.ShapeDtypeStruct((dec_batch * dec_seq, n_heads * hd), BF16),
        compiler_params=_params("parallel", "parallel"),
    )(z, mem_k, mem_v)


def _merge_kernel(oa_ref, ob_ref, oc_ref, wa_ref, wb_ref, wc_ref, ga_ref, gb_ref, gc_ref, m_ref):
    merged = jax.nn.sigmoid(ga_ref[...]) * _dot(oa_ref[...], wa_ref[...])
    merged = merged + jax.nn.sigmoid(gb_ref[...]) * _dot(ob_ref[...], wb_ref[...])
    merged = merged + jax.nn.sigmoid(gc_ref[...]) * _dot(oc_ref[...], wc_ref[...])
    m_ref[...] = merged.astype(BF16)


def _merge(o_a, o_b, o_c, w_a, w_b, w_c, z, *, tm, tn, gz_off):
    t = o_a.shape[0]
    d = w_a.shape[1]
    nj = d // tn
    act = lambda a: pl.BlockSpec((tm, a.shape[1]), lambda i, j: (i, 0))
    wsp = lambda w: pl.BlockSpec((w.shape[0], tn), lambda i, j: (0, j))
    gate = lambda br: pl.BlockSpec((tm, tn), lambda i, j: (i, (gz_off + br * d) // tn + j))
    return pl.pallas_call(
        _merge_kernel,
        grid=(t // tm, nj),
        in_specs=[act(o_a), act(o_b), act(o_c), wsp(w_a), wsp(w_b), wsp(w_c), gate(0), gate(1), gate(2)],
        out_specs=pl.BlockSpec((tm, tn), lambda i, j: (i, j)),
        out_shape=jax.ShapeDtypeStruct((t, d), BF16),
        compiler_params=_params("parallel", "parallel"),
    )(o_a, o_b, o_c, w_a, w_b, w_c, z, z, z)


def _out_proj_kernel(x_ref, m_ref, wo_ref, nw_ref, wr_ref, br_ref, x1_ref, h_ref, lg_ref):
    x1 = x_ref[...] + _dot(m_ref[...], wo_ref[...])
    x1_ref[...] = x1
    h = _rms(x1, nw_ref[...])
    h_ref[...] = h
    lg_ref[...] = jnp.dot(h, wr_ref[...], preferred_element_type=F32, precision=lax.Precision.HIGHEST) + br_ref[...]


def _out_proj(x, merged, w_o, norm_w, w_router, b_router, *, tm):
    t, d = x.shape
    row = lambda i: (i, 0)
    const = lambda i: (0, 0)
    return pl.pallas_call(
        _out_proj_kernel,
        grid=(t // tm,),
        in_specs=[
            pl.BlockSpec((tm, d), row),
            pl.BlockSpec((tm, d), row),
            pl.BlockSpec((d, d), const),
            pl.BlockSpec((1, d), const),
            pl.BlockSpec((d, LANES), const),
            pl.BlockSpec((1, LANES), const),
        ],
        out_specs=[pl.BlockSpec((tm, d), row), pl.BlockSpec((tm, d), row), pl.BlockSpec((tm, LANES), row)],
        out_shape=[
            jax.ShapeDtypeStruct((t, d), F32),
            jax.ShapeDtypeStruct((t, d), F32),
            jax.ShapeDtypeStruct((t, LANES), F32),
        ],
        compiler_params=_params("parallel"),
    )(x, merged, w_o, norm_w, w_router, b_router)


def _gather_rows(src_hbm, idx_ref, base, buf, sem, n_rows):
    def issue(r, carry):
        pltpu.make_async_copy(src_hbm.at[pl.ds(idx_ref[base + r], 1), :], buf.at[pl.ds(r, 1), :], sem).start()
        return carry

    lax.fori_loop(0, n_rows, issue, 0)


def _wait_rows(src_hbm, buf, sem, n_rows):
    def wait(r, carry):
        pltpu.make_async_copy(src_hbm.at[pl.ds(0, 1), :], buf.at[pl.ds(r, 1), :], sem).wait()
        return carry

    lax.fori_loop(0, n_rows, wait, 0)


def _experts_kernel(be_ref, nu_ref, tok_ref, h_hbm, wg_ref, wu_ref, wd_ref, y_ref, buf, sems, *, blk):
    i = pl.program_id(0)
    n_used = nu_ref[0]
    slot = i % 2

    @pl.when(i == 0)
    def _():
        _gather_rows(h_hbm, tok_ref, 0, buf.at[0], sems.at[0], blk)

    @pl.when(i + 1 < n_used)
    def _():
        _gather_rows(h_hbm, tok_ref, (i + 1) * blk, buf.at[1 - slot], sems.at[1 - slot], blk)

    @pl.when(i < n_used)
    def _():
        _wait_rows(h_hbm, buf.at[slot], sems.at[slot], blk)
        xb = buf[slot].astype(BF16)
        g = _dot(xb, wg_ref[...])
        u = _dot(xb, wu_ref[...])
        a = (g * jax.nn.sigmoid(g) * u).astype(BF16)
        y_ref[...] = _dot(a, wd_ref[...])

    @pl.when(i >= n_used)
    def _():
        y_ref[...] = jnp.zeros(y_ref.shape, F32)


def _experts(blk_expert, n_used, row_tok, h, w_gate, w_up, w_down, *, n_blk, blk):
    d = h.shape[1]
    de = w_gate.shape[2]
    grid_spec = pltpu.PrefetchScalarGridSpec(
        num_scalar_prefetch=3,
        grid=(n_blk,),
        in_specs=[
            pl.BlockSpec(memory_space=pl.ANY),
            pl.BlockSpec((None, d, de), lambda i, be, nu, tok: (be[i], 0, 0)),
            pl.BlockSpec((None, d, de), lambda i, be, nu, tok: (be[i], 0, 0)),
            pl.BlockSpec((None, de, d), lambda i, be, nu, tok: (be[i], 0, 0)),
        ],
        out_specs=pl.BlockSpec((blk, d), lambda i, be, nu, tok: (i, 0)),
        scratch_shapes=[pltpu.VMEM((2, blk, d), F32), pltpu.SemaphoreType.DMA((2,))],
    )
    return pl.pallas_call(
        functools.partial(_experts_kernel, blk=blk),
        grid_spec=grid_spec,
        out_shape=jax.ShapeDtypeStruct((n_blk * blk, d), F32),
        compiler_params=_params("arbitrary"),
    )(blk_expert, n_used, row_tok, h, w_gate, w_up, w_down)


def _combine_kernel(d0_ref, d1_ref, y_hbm, x1_ref, g0_ref, g1_ref, nw_ref, o_ref, buf0, buf1, sems, *, tm):
    i = pl.program_id(0)
    slot = i % 2

    def issue(step, s):
        _gather_rows(y_hbm, d0_ref, step * tm, buf0.at[s], sems.at[0, s], tm)
        _gather_rows(y_hbm, d1_ref, step * tm, buf1.at[s], sems.at[1, s], tm)

    @pl.when(i == 0)
    def _():
        issue(0, 0)

    @pl.when(i + 1 < pl.num_programs(0))
    def _():
        issue(i + 1, 1 - slot)

    _wait_rows(y_hbm, buf0.at[slot], sems.at[0, slot], tm)
    _wait_rows(y_hbm, buf1.at[slot], sems.at[1, slot], tm)
    reps = x1_ref.shape[1] // LANES
    g0 = jnp.tile(g0_ref[...], (1, reps))
    g1 = jnp.tile(g1_ref[...], (1, reps))
    x2 = x1_ref[...] + (buf0[slot] * g0 + buf1[slot] * g1)
    o_ref[...] = _rms(x2, nw_ref[...])


def _combine(d0, d1, y_rows, x1, g0, g1, norm_w, *, tm):
    t, d = x1.shape
    row = lambda i, a, b: (i, 0)
    grid_spec = pltpu.PrefetchScalarGridSpec(
        num_scalar_prefetch=2,
        grid=(t // tm,),
        in_specs=[
            pl.BlockSpec(memory_space=pl.ANY),
            pl.BlockSpec((tm, d), row),
            pl.BlockSpec((tm, LANES), row),
            pl.BlockSpec((tm, LANES), row),
            pl.BlockSpec((1, d), lambda i, a, b: (0, 0)),
        ],
        out_specs=pl.BlockSpec((tm, d), row),
        scratch_shapes=[
            pltpu.VMEM((2, tm, d), F32),
            pltpu.VMEM((2, tm, d), F32),
            pltpu.SemaphoreType.DMA((2, 2)),
        ],
    )
    return pl.pallas_call(
        functools.partial(_combine_kernel, tm=tm),
        grid_spec=grid_spec,
        out_shape=jax.ShapeDtypeStruct((t, d), F32),
        compiler_params=_params("arbitrary"),
    )(d0, d1, y_rows, x1, g0, g1, norm_w)


def _route(logits, n_groups, per_group, top_k, blk):
    t = logits.shape[0]
    n_exp = n_groups * per_group
    grp_p = jax.nn.softmax(logits[:, :n_groups], axis=-1)
    g_prob, g_idx = lax.top_k(grp_p, 1)
    e_logit = logits[:, n_groups:n_groups + n_exp].reshape(t, n_groups, per_group)
    in_grp = jnp.take_along_axis(e_logit, g_idx[:, :, None], axis=1)[:, 0, :]
    top_v, top_i = lax.top_k(in_grp, top_k)
    gate = g_prob * jax.nn.softmax(top_v, axis=-1)
    expert = (g_idx * per_group + top_i).reshape(-1).astype(jnp.int32)
    a = t * top_k
    onehot = expert[:, None] == jnp.arange(n_exp, dtype=jnp.int32)[None, :]
    csum = jnp.cumsum(onehot.astype(jnp.int32), axis=0)
    counts = csum[-1]
    rank = jnp.sum(jnp.where(onehot, csum - 1, 0), axis=1)
    padded = (counts + blk - 1) // blk * blk
    pad_end = jnp.cumsum(padded)
    pad_start = pad_end - padded
    dest = (pad_start[expert] + rank).astype(jnp.int32)
    n_blk = (a + n_exp * (blk - 1) + blk - 1) // blk
    tok = jnp.arange(a, dtype=jnp.int32) // top_k
    row_tok = jnp.zeros((n_blk * blk,), jnp.int32).at[dest].set(tok)
    blk_start = jnp.arange(n_blk, dtype=jnp.int32) * blk
    blk_expert = jnp.minimum(jnp.sum(pad_end[None, :] <= blk_start[:, None], axis=1), n_exp - 1).astype(jnp.int32)
    n_used = (pad_end[-1:] // blk).astype(jnp.int32)
    return gate, dest.reshape(t, top_k), row_tok, blk_expert, n_used, n_blk


def _pick(n, pref):
    return pref if n % pref == 0 else n


def kernel(x_prompt, x_sample, mem_prompt, cache_ckv, cache_krope, state_ret, cache_mem_k, cache_mem_v, page_table, attn_norm_w, w_in, mla_q_norm_w, mla_w_uq, mla_kv_norm_w, mla_w_uk, mla_w_uv, ret_gn_w, mem_norm_w, mem_w_kv, w_branch_a, w_branch_b, w_branch_c, w_out, ffn_norm_w, router_grp_w, router_grp_b, router_exp_w, router_exp_b, exp_w_gate, exp_w_up, exp_w_down, final_norm_w):
    batch, seq, d = x_prompt.shape
    dec_batch, dec_seq, _ = x_sample.shape
    depth = w_in.shape[0]
    q_rank, n_heads, qh = mla_w_uq.shape[1:]
    kv_rank, _, nope = mla_w_uk.shape[1:]
    rope_dim = qh - nope
    v_dim = mla_w_uv.shape[3]
    ret_heads, ret_dv = ret_gn_w.shape[1:]
    ret_dk = state_ret.shape[3]
    mem_tokens = mem_prompt.shape[1]
    mem_heads, mem_hd = cache_mem_k.shape[3:]
    mem_width = mem_heads * mem_hd
    n_groups = router_grp_w.shape[2]
    n_exp = router_exp_w.shape[2]
    per_group = n_exp // n_groups
    top_k = 2
    page = cache_ckv.shape[2]
    past_len = page_table.shape[1] * page
    assert nope == LANES and v_dim == LANES and rope_dim == 64 and ret_dk == LANES
    assert q_rank == kv_rank and n_groups + n_exp <= LANES

    tp = batch * seq
    ts = dec_batch * dec_seq
    t = tp + ts
    ret_qk = ret_heads * ret_dk
    ret_v = ret_heads * ret_dv
    cq_off, ckv_off = 0, q_rank
    rq_off = ckv_off + kv_rank
    rk_off = rq_off + ret_qk
    rv_off = rk_off + ret_qk
    rg_off = rv_off + ret_v
    mq_off = rg_off + ret_v
    gz_off = mq_off + mem_width
    ret_offs = (rq_off, rk_off, rv_off, rg_off)

    pos = jnp.concatenate([jnp.tile(jnp.arange(seq), batch), jnp.tile(past_len + jnp.arange(dec_seq), dec_batch)])
    cos64, sin64 = _rope_tables(pos, rope_dim)
    cos128, sin128 = _rope_tables(pos, ret_dk)
    lgam = jnp.log1p(-jnp.exp2(-5.0 - jnp.arange(ret_heads, dtype=F32)))
    lgam = jnp.broadcast_to(lgam[:, None, None], (ret_heads, 1, LANES))

    xp = x_prompt.reshape(tp, d)
    xs = x_sample.reshape(ts, d)
    x = jnp.concatenate([xp, xs], axis=0)

    tm_big = _pick(t, 512)
    tm_mid = _pick(t, 256)
    outs = {k: [] for k in ("p_ckv", "p_kr", "p_ret", "p_mk", "p_mv", "s_ckv", "s_kr", "s_ret")}
    for l in range(depth):
        wi = w_in[l]
        kr_lo = q_rank + kv_rank
        w_main = jnp.concatenate([wi[:, :kr_lo], wi[:, kr_lo + rope_dim:]], axis=1).astype(BF16)
        w_kr = jnp.pad(wi[:, kr_lo:kr_lo + rope_dim], ((0, 0), (0, LANES - rope_dim))).astype(BF16)
        z, zk = _norm_proj(x, attn_norm_w[l], w_main, w_kr, tm=tm_big, tn=_pick(w_main.shape[1], 1024))

        wuq = mla_w_uq[l]
        wqn = wuq[:, :, :nope].reshape(q_rank, n_heads * LANES).astype(BF16)
        wqr = jnp.pad(wuq[:, :, nope:], ((0, 0), (0, 0), (0, LANES - rope_dim))).reshape(q_rank, n_heads * LANES).astype(BF16)
        wuk = mla_w_uk[l].reshape(kv_rank, n_heads * nope).astype(BF16)
        wuv = mla_w_uv[l].reshape(kv_rank, n_heads * v_dim).astype(BF16)
        qcat, c, kr, kcat, v = _mla_prep(
            z, zk, cos64, sin64, mla_q_norm_w[l].reshape(1, -1), mla_kv_norm_w[l].reshape(1, -1), wqn, wqr, wuk, wuv,
            tm=tm_mid, n_heads=n_heads, q_rank=q_rank, kv_rank=kv_rank, rope_dim=rope_dim)

        scale = qh ** -0.5
        oa_p = _mla_prompt(qcat, kcat, v, batch=batch, seq=seq, n_heads=n_heads, tq=_pick(seq, 256), scale=scale)
        wukt = jnp.transpose(mla_w_uk[l], (1, 2, 0)).astype(BF16)
        qlat, qr = _absorb_q(qcat, wukt, ts=ts, row_block=tp // ts, n_heads=n_heads, kv_rank=kv_rank)
        rows = dec_seq * n_heads
        new_pad = LANES - dec_seq
        cnew = jnp.pad(c[tp:].reshape(dec_batch, dec_seq, kv_rank), ((0, 0), (0, new_pad), (0, 0))).astype(BF16)
        krnew = jnp.pad(kr[tp:].reshape(dec_batch, dec_seq, rope_dim), ((0, 0), (0, new_pad), (0, 0))).astype(BF16)
        o_lat = _mla_decode(
            page_table, qlat.reshape(dec_batch, rows, kv_rank), qr.reshape(dec_batch, rows, LANES), cnew, krnew,
            cache_ckv[l], cache_krope[l], npg=_pick(page_table.shape[1], 8), scale=scale, n_heads=n_heads,
            rope_dim=rope_dim)
        wuv_h = jnp.transpose(mla_w_uv[l], (1, 0, 2)).astype(BF16)
        oa_s = _head_proj(o_lat.reshape(ts, n_heads * kv_rank), wuv_h, n_heads=n_heads)
        o_a = jnp.concatenate([oa_p, oa_s], axis=0)

        gn_w = ret_gn_w[l].reshape(1, ret_v)
        ob_p, st_p = _ret_prompt(z, cos128, sin128, lgam, gn_w, batch=batch, seq=seq, n_heads=ret_heads, dk=ret_dk,
                                 dv=ret_dv, rows=_pick(seq, 512), offs=ret_offs)
        ob_s, st_s = _ret_sample(z, cos128, sin128, lgam, gn_w, state_ret[l], row0=tp, dec_batch=dec_batch,
                                 dec_seq=dec_seq, n_heads=ret_heads, dk=ret_dk, dv=ret_dv, offs=ret_offs)
        o_b = jnp.concatenate([ob_p, ob_s], axis=0)

        mem_rows = batch * mem_tokens
        (kv,) = _norm_proj(mem_prompt.reshape(mem_rows, d), mem_norm_w[l], mem_w_kv[l].astype(BF16),
                           tm=_pick(mem_rows, 512), tn=_pick(2 * mem_width, 1024))
        oc_p = _mem_attn_prompt(z, kv, batch=batch, seq=seq, mem_tokens=mem_tokens, n_heads=mem_heads, hd=mem_hd,
                                tq=_pick(seq, 512), mq_off=mq_off)
        oc_s = _mem_attn_sample(z, cache_mem_k[l].reshape(dec_batch, mem_tokens, mem_width),
                                cache_mem_v[l].reshape(dec_batch, mem_tokens, mem_width), row0=tp,
                                dec_batch=dec_batch, dec_seq=dec_seq, n_heads=mem_heads, hd=mem_hd, nb=8,
                                mq_off=mq_off)
        o_c = jnp.concatenate([oc_p, oc_s], axis=0)

        merged = _merge(o_a, o_b, o_c, w_branch_a[l].astype(BF16), w_branch_b[l].astype(BF16),
                        w_branch_c[l].astype(BF16), z, tm=tm_big, tn=_pick(d, 512), gz_off=gz_off)
        n_route = n_groups + n_exp
        w_router = jnp.pad(jnp.concatenate([router_grp_w[l], router_exp_w[l]], axis=1), ((0, 0), (0, LANES - n_route)))
        b_router = jnp.pad(jnp.concatenate([router_grp_b[l], router_exp_b[l]]), (0, LANES - n_route)).reshape(1, LANES)
        x1, h2, logits = _out_proj(x, merged, w_out[l].astype(BF16), ffn_norm_w[l].reshape(1, d), w_router, b_router,
                                   tm=tm_mid)

        gate, dest, row_tok, blk_expert, n_used, n_blk = _route(logits, n_groups, per_group, top_k, MOE_BLOCK)
        y_rows = _experts(blk_expert, n_used, row_tok, h2, exp_w_gate[l].astype(BF16), exp_w_up[l].astype(BF16),
                          exp_w_down[l].astype(BF16), n_blk=n_blk, blk=MOE_BLOCK)
        g0 = jnp.broadcast_to(gate[:, 0:1], (t, LANES))
        g1 = jnp.broadcast_to(gate[:, 1:2], (t, LANES))
        last = l == depth - 1
        norm_w = final_norm_w if last else jnp.ones((d,), F32)
        x_next = _combine(dest[:, 0], dest[:, 1], y_rows, x1, g0, g1, norm_w.reshape(1, d), tm=_pick(t, 128))
        assert last, "depth > 1 needs the un-normalised residual stream carried between layers"
        x = x_next

        outs["p_ckv"].append(c[:tp].reshape(batch, seq, kv_rank))
        outs["p_kr"].append(kr[:tp].reshape(batch, seq, rope_dim))
        outs["p_ret"].append(st_p)
        outs["p_mk"].append(kv[:, :mem_width].reshape(batch, mem_tokens, mem_heads, mem_hd))
        outs["p_mv"].append(kv[:, mem_width:].reshape(batch, mem_tokens, mem_heads, mem_hd))
        outs["s_ckv"].append(c[tp:].reshape(dec_batch, dec_seq, kv_rank))
        outs["s_kr"].append(kr[tp:].reshape(dec_batch, dec_seq, rope_dim))
        outs["s_ret"].append(st_s)

    y_prompt = x[:tp].reshape(batch, seq, d)
    y_sample = x[tp:].reshape(dec_batch, dec_seq, d)
    st = lambda k: jnp.stack(outs[k])
    return (y_prompt, y_sample, st("p_ckv"), st("p_kr"), st("p_ret"), st("p_mk"), st("p_mv"), st("s_ckv"),
            st("s_kr"), st("s_ret"))
```

```python
import functools

import jax
import jax.numpy as jnp
from jax import lax
from jax.experimental import pallas as pl
from jax.experimental.pallas import tpu as pltpu

F32 = jnp.float32
BF16 = jnp.bfloat16

NORM_EPS = 1e-6
GN_EPS = 1e-5
ROPE_THETA = 10000.0
RET_CHUNK = 128
MOE_BLOCK = 128
LANES = 128
NEG_BIG = -1e30
VMEM_LIMIT_BYTES = 56 * 1024 * 1024


def _params(*sem):
    return pltpu.CompilerParams(dimension_semantics=sem, vmem_limit_bytes=VMEM_LIMIT_BYTES)


def _dot(a, b):
    return jnp.dot(a, b, preferred_element_type=F32)


def _dot_nt(a, b):
    return lax.dot_general(a, b, (((1,), (1,)), ((), ())), preferred_element_type=F32)


def _rms(x, w):
    return x * lax.rsqrt(jnp.mean(x * x, axis=-1, keepdims=True) + NORM_EPS) * w


def _rope64(x, cos, sin):
    lane = lax.broadcasted_iota(jnp.int32, x.shape, 1)
    first = (lane & 63) < 32
    swapped = jnp.where(first, pltpu.roll(x, 96, 1), pltpu.roll(x, 32, 1))
    return x * cos + swapped * sin


def _rope128(x, cos, sin):
    return x * cos + pltpu.roll(x, 64, 1) * sin


def _rope_tables(pos, dim):
    half = dim // 2
    inv = ROPE_THETA ** (-jnp.arange(half, dtype=F32) / half)
    ang = pos.astype(F32)[:, None] * inv[None, :]
    cos = jnp.cos(ang)
    sin = jnp.sin(ang)
    c = jnp.concatenate([cos, cos], axis=-1)
    s = jnp.concatenate([-sin, sin], axis=-1)
    reps = LANES // dim
    return jnp.tile(c, (1, reps)), jnp.tile(s, (1, reps))


def _norm_proj_kernel(x_ref, nw_ref, w_ref, z_ref, h_ref):
    @pl.when(pl.program_id(1) == 0)
    def _():
        h_ref[...] = _rms(x_ref[...], nw_ref[...]).astype(BF16)

    z_ref[...] = _dot(h_ref[...], w_ref[...])


def _norm_proj_extra_kernel(x_ref, nw_ref, w_ref, we_ref, z_ref, ze_ref, h_ref):
    @pl.when(pl.program_id(1) == 0)
    def _():
        h = _rms(x_ref[...], nw_ref[...]).astype(BF16)
        h_ref[...] = h
        ze_ref[...] = _dot(h, we_ref[...])

    z_ref[...] = _dot(h_ref[...], w_ref[...])


def _norm_proj(x, norm_w, w, w_extra=None, *, tm, tn):
    t, d = x.shape
    n = w.shape[1]
    grid = (t // tm, n // tn)
    in_specs = [
        pl.BlockSpec((tm, d), lambda i, j: (i, 0)),
        pl.BlockSpec((1, d), lambda i, j: (0, 0)),
        pl.BlockSpec((d, tn), lambda i, j: (0, j)),
    ]
    out_specs = [pl.BlockSpec((tm, tn), lambda i, j: (i, j))]
    out_shape = [jax.ShapeDtypeStruct((t, n), F32)]
    args = [x, norm_w.reshape(1, d), w]
    body = _norm_proj_kernel
    if w_extra is not None:
        ne = w_extra.shape[1]
        in_specs.append(pl.BlockSpec((d, ne), lambda i, j: (0, 0)))
        out_specs.append(pl.BlockSpec((tm, ne), lambda i, j: (i, 0)))
        out_shape.append(jax.ShapeDtypeStruct((t, ne), F32))
        args.append(w_extra)
        body = _norm_proj_extra_kernel
    return pl.pallas_call(
        body,
        grid=grid,
        in_specs=in_specs,
        out_specs=out_specs,
        out_shape=out_shape,
        scratch_shapes=[pltpu.VMEM((tm, d), BF16)],
        compiler_params=_params("parallel", "arbitrary"),
    )(*args)


def _mla_prep_kernel(cq_ref, ckv_ref, zk_ref, cos_ref, sin_ref, qnw_ref, kvnw_ref, wqn_ref, wqr_ref,
                     wuk_ref, wuv_ref, qcat_ref, c_ref, kr_ref, kcat_ref, v_ref, *, n_heads, rope_dim):
    cos = cos_ref[...]
    sin = sin_ref[...]
    cqn = _rms(cq_ref[...], qnw_ref[...]).astype(BF16)
    qn = _dot(cqn, wqn_ref[...])
    qr = _dot(cqn, wqr_ref[...])
    c = _rms(ckv_ref[...], kvnw_ref[...])
    c_ref[...] = c
    cb = c.astype(BF16)
    kn = _dot(cb, wuk_ref[...])
    v_ref[...] = _dot(cb, wuv_ref[...]).astype(BF16)
    kr = _rope64(zk_ref[...], cos, sin)
    kr_ref[...] = kr[:, :rope_dim]
    krb = kr.astype(BF16)
    for h in range(n_heads):
        lo = h * 2 * LANES
        sl = slice(h * LANES, (h + 1) * LANES)
        qcat_ref[:, lo:lo + LANES] = qn[:, sl].astype(BF16)
        qcat_ref[:, lo + LANES:lo + 2 * LANES] = _rope64(qr[:, sl], cos, sin).astype(BF16)
        kcat_ref[:, lo:lo + LANES] = kn[:, sl].astype(BF16)
        kcat_ref[:, lo + LANES:lo + 2 * LANES] = krb


def _mla_prep(z, zk, cos64, sin64, qnw, kvnw, wqn, wqr, wuk, wuv, *, tm, n_heads, q_rank, kv_rank, rope_dim):
    t = z.shape[0]
    row = lambda i: (i, 0)
    const = lambda i: (0, 0)
    hw = n_heads * LANES
    return pl.pallas_call(
        functools.partial(_mla_prep_kernel, n_heads=n_heads, rope_dim=rope_dim),
        grid=(t // tm,),
        in_specs=[
            pl.BlockSpec((tm, q_rank), lambda i: (i, 0)),
            pl.BlockSpec((tm, kv_rank), lambda i: (i, q_rank // kv_rank)),
            pl.BlockSpec((tm, LANES), row),
            pl.BlockSpec((tm, LANES), row),
            pl.BlockSpec((tm, LANES), row),
            pl.BlockSpec((1, q_rank), const),
            pl.BlockSpec((1, kv_rank), const),
            pl.BlockSpec((q_rank, hw), const),
            pl.BlockSpec((q_rank, hw), const),
            pl.BlockSpec((kv_rank, hw), const),
            pl.BlockSpec((kv_rank, hw), const),
        ],
        out_specs=[
            pl.BlockSpec((tm, 2 * hw), row),
            pl.BlockSpec((tm, kv_rank), row),
            pl.BlockSpec((tm, rope_dim), row),
            pl.BlockSpec((tm, 2 * hw), row),
            pl.BlockSpec((tm, hw), row),
        ],
        out_shape=[
            jax.ShapeDtypeStruct((t, 2 * hw), BF16),
            jax.ShapeDtypeStruct((t, kv_rank), F32),
            jax.ShapeDtypeStruct((t, rope_dim), F32),
            jax.ShapeDtypeStruct((t, 2 * hw), BF16),
            jax.ShapeDtypeStruct((t, hw), BF16),
        ],
        compiler_params=_params("parallel"),
    )(z, z, zk, cos64, sin64, qnw, kvnw, wqn, wqr, wuk, wuv)


def _flash_kernel(q_ref, k_ref, v_ref, o_ref, *, tq, scale):
    qi = pl.program_id(2)
    q = q_ref[...]

    def block(j, carry, diagonal):
        m, l, acc = carry
        start = pl.multiple_of(j * tq, tq)
        k = k_ref[pl.ds(start, tq), :]
        v = v_ref[pl.ds(start, tq), :]
        s = _dot_nt(q, k) * scale
        if diagonal:
            rows = lax.broadcasted_iota(jnp.int32, (tq, tq), 0)
            cols = lax.broadcasted_iota(jnp.int32, (tq, tq), 1)
            s = jnp.where(cols <= rows, s, NEG_BIG)
        m_new = jnp.maximum(m, jnp.max(s, axis=-1, keepdims=True))
        alpha = jnp.exp(m - m_new)
        p = jnp.exp(s - m_new)
        l = alpha * l + jnp.sum(p, axis=-1, keepdims=True)
        acc = alpha * acc + _dot(p.astype(BF16), v)
        return m_new, l, acc

    dv = v_ref.shape[-1]
    init = (jnp.full((tq, 1), NEG_BIG, F32), jnp.zeros((tq, 1), F32), jnp.zeros((tq, dv), F32))
    carry = lax.fori_loop(0, qi, functools.partial(block, diagonal=False), init)
    _, l, acc = block(qi, carry, True)
    o_ref[...] = (acc / l).astype(BF16)


def _mla_prompt(qcat, kcat, v, *, batch, seq, n_heads, tq, scale, total_rows):
    nq = seq // tq
    dv = v.shape[1] // n_heads
    return pl.pallas_call(
        functools.partial(_flash_kernel, tq=tq, scale=scale),
        grid=(batch, n_heads, nq),
        in_specs=[
            pl.BlockSpec((tq, 2 * LANES), lambda b, h, i: (b * nq + i, h)),
            pl.BlockSpec((seq, 2 * LANES), lambda b, h, i: (b, h)),
            pl.BlockSpec((seq, dv), lambda b, h, i: (b, h)),
        ],
        out_specs=pl.BlockSpec((tq, dv), lambda b, h, i: (b * nq + i, h)),
        out_shape=jax.ShapeDtypeStruct((total_rows, n_heads * dv), BF16),
        compiler_params=_params("parallel", "parallel", "arbitrary"),
    )(qcat, kcat, v)


def _absorb_q_kernel(qcat_ref, wukt_ref, qlat_ref, qr_ref, *, n_heads, kv_rank):
    for h in range(n_heads):
        lo = h * 2 * LANES
        qlat_ref[:, h * kv_rank:(h + 1) * kv_rank] = _dot(qcat_ref[:, lo:lo + LANES], wukt_ref[h]).astype(BF16)
        qr_ref[:, h * LANES:(h + 1) * LANES] = qcat_ref[:, lo + LANES:lo + 2 * LANES]


def _absorb_q(qcat, wukt, *, ts, row_block, n_heads, kv_rank):
    return pl.pallas_call(
        functools.partial(_absorb_q_kernel, n_heads=n_heads, kv_rank=kv_rank),
        grid=(1,),
        in_specs=[
            pl.BlockSpec((ts, qcat.shape[1]), lambda i: (row_block, 0)),
            pl.BlockSpec(wukt.shape, lambda i: (0, 0, 0)),
        ],
        out_specs=[
            pl.BlockSpec((ts, n_heads * kv_rank), lambda i: (0, 0)),
            pl.BlockSpec((ts, n_heads * LANES), lambda i: (0, 0)),
        ],
        out_shape=[
            jax.ShapeDtypeStruct((ts, n_heads * kv_rank), BF16),
            jax.ShapeDtypeStruct((ts, n_heads * LANES), BF16),
        ],
        compiler_params=_params("arbitrary"),
    )(qcat, wukt)


def _decode_kernel(pt_ref, qlat_ref, qr_ref, cnew_ref, krnew_ref, ckv_hbm, krt_hbm, o_ref, cbuf, kbuf, cb_ref, kb_ref,
                   m_ref, l_ref, acc_ref, sems, *, npg, scale, n_heads, rope_dim):
    b = pl.program_id(0)
    step = pl.program_id(1)
    steps = pl.num_programs(1)
    g_step = b * steps + step
    slot = g_step % 2
    page = cbuf.shape[2]

    def fetch(bb, ss, sl):
        def body(j, carry):
            pg = pt_ref[bb, ss * npg + j]
            pltpu.make_async_copy(ckv_hbm.at[pg], cbuf.at[sl, j], sems.at[0, sl]).start()
            pltpu.make_async_copy(krt_hbm.at[pg], kbuf.at[sl, j], sems.at[1, sl]).start()
            return carry

        lax.fori_loop(0, npg, body, 0, unroll=8)

    @pl.when(g_step == 0)
    def _():
        fetch(0, 0, 0)

    @pl.when(g_step + 1 < pl.num_programs(0) * steps)
    def _():
        wrap = step + 1 == steps
        fetch(jnp.where(wrap, b + 1, b), jnp.where(wrap, 0, step + 1), 1 - slot)

    pltpu.make_async_copy(ckv_hbm.at[pl.ds(0, npg)], cbuf.at[slot], sems.at[0, slot]).wait()
    pltpu.make_async_copy(krt_hbm.at[pl.ds(0, npg)], kbuf.at[slot], sems.at[1, slot]).wait()

    @pl.when(step == 0)
    def _():
        m_ref[...] = jnp.full(m_ref.shape, NEG_BIG, F32)
        l_ref[...] = jnp.zeros(l_ref.shape, F32)
        acc_ref[...] = jnp.zeros(acc_ref.shape, F32)

    ql = qlat_ref[0]
    qr = qr_ref[0][:, :rope_dim]

    def update(s, vals):
        m_prev = m_ref[...]
        m_new = jnp.maximum(m_prev, jnp.max(s, axis=-1, keepdims=True))
        alpha = jnp.exp(m_prev - m_new)
        p = jnp.exp(s - m_new)
        l_ref[...] = alpha * l_ref[...] + jnp.sum(p, axis=-1, keepdims=True)
        acc_ref[...] = alpha * acc_ref[...] + _dot(p.astype(BF16), vals)
        m_ref[...] = m_new

    for j in range(npg):
        cb_ref[j * page:(j + 1) * page, :] = cbuf[slot, j].astype(BF16)
        kb_ref[:, j * page:(j + 1) * page] = kbuf[slot, j].astype(BF16)
    cb = cb_ref[...]
    update((_dot_nt(ql, cb) + _dot(qr, kb_ref[...])) * scale, cb)

    @pl.when(step == pl.num_programs(1) - 1)
    def _():
        cn = cnew_ref[0]
        krn = krnew_ref[0]
        s = (_dot_nt(ql, cn) + _dot_nt(qr, krn)) * scale
        t_row = lax.broadcasted_iota(jnp.int32, s.shape, 0) >> (n_heads.bit_length() - 1)
        col = lax.broadcasted_iota(jnp.int32, s.shape, 1)
        update(jnp.where(col <= t_row, s, NEG_BIG), cn)
        o_ref[0] = (acc_ref[...] / l_ref[...]).astype(BF16)


def _mla_decode(page_table, qlat, qr, cnew, krnew, cache_ckv, cache_krope_t, *, npg, scale, n_heads, rope_dim):
    nb, rows, kv_rank = qlat.shape
    n_pages = page_table.shape[1]
    page = cache_ckv.shape[1]
    steps = n_pages // npg
    per_b = lambda b, p, pt: (b, 0, 0)
    grid_spec = pltpu.PrefetchScalarGridSpec(
        num_scalar_prefetch=1,
        grid=(nb, steps),
        in_specs=[
            pl.BlockSpec((1, rows, kv_rank), per_b),
            pl.BlockSpec((1, rows, LANES), per_b),
            pl.BlockSpec((1,) + cnew.shape[1:], per_b),
            pl.BlockSpec((1,) + krnew.shape[1:], per_b),
            pl.BlockSpec(memory_space=pl.ANY),
            pl.BlockSpec(memory_space=pl.ANY),
        ],
        out_specs=pl.BlockSpec((1, rows, kv_rank), per_b),
        scratch_shapes=[
            pltpu.VMEM((2, npg, page, kv_rank), F32),
            pltpu.VMEM((2, npg, rope_dim, page), F32),
            pltpu.VMEM((npg * page, kv_rank), BF16),
            pltpu.VMEM((rope_dim, npg * page), BF16),
            pltpu.VMEM((rows, 1), F32),
            pltpu.VMEM((rows, 1), F32),
            pltpu.VMEM((rows, kv_rank), F32),
            pltpu.SemaphoreType.DMA((2, 2)),
        ],
    )
    return pl.pallas_call(
        functools.partial(_decode_kernel, npg=npg, scale=scale, n_heads=n_heads, rope_dim=rope_dim),
        grid_spec=grid_spec,
        out_shape=jax.ShapeDtypeStruct((nb, rows, kv_rank), BF16),
        compiler_params=_params("arbitrary", "arbitrary"),
    )(page_table, qlat, qr, cnew, krnew, cache_ckv, cache_krope_t)


def _head_proj_kernel(x_ref, w_ref, o_ref, *, n_heads):
    k = x_ref.shape[1] // n_heads
    n = o_ref.shape[1] // n_heads
    for h in range(n_heads):
        o_ref[:, h * n:(h + 1) * n] = _dot(x_ref[:, h * k:(h + 1) * k], w_ref[h]).astype(BF16)


def _head_proj(x, w, *, n_heads):
    rows = x.shape[0]
    n = w.shape[2]
    return pl.pallas_call(
        functools.partial(_head_proj_kernel, n_heads=n_heads),
        grid=(1,),
        in_specs=[pl.BlockSpec(x.shape, lambda i: (0, 0)), pl.BlockSpec(w.shape, lambda i: (0, 0, 0))],
        out_specs=pl.BlockSpec((rows, n_heads * n), lambda i: (0, 0)),
        out_shape=jax.ShapeDtypeStruct((rows, n_heads * n), BF16),
        compiler_params=_params("arbitrary"),
    )(x, w)


def _group_norm_gate(o, rg, gn_w):
    mu = jnp.mean(o, axis=-1, keepdims=True)
    d = o - mu
    var = jnp.mean(d * d, axis=-1, keepdims=True)
    on = d * lax.rsqrt(var + GN_EPS) * gn_w
    return (on * (rg * jax.nn.sigmoid(rg))).astype(BF16)


def _ret_prompt_kernel(q_ref, k_ref, v_ref, g_ref, cos_ref, sin_ref, lg_ref, gn_ref, o_ref, st_ref, state,
                       *, n_sub, chunk, k_scale):
    ci = pl.program_id(2)

    @pl.when(ci == 0)
    def _():
        state[...] = jnp.zeros(state.shape, F32)

    lg = lg_ref[0][:, :1]
    ii = lax.broadcasted_iota(jnp.int32, (chunk, chunk), 0)
    jj = lax.broadcasted_iota(jnp.int32, (chunk, chunk), 1)
    diff = (ii - jj).astype(F32)
    decay = jnp.where(diff >= 0, jnp.exp(jnp.maximum(diff, 0.0) * lg), 0.0)
    ri = lax.broadcasted_iota(jnp.int32, (chunk, 1), 0).astype(F32)
    xi = jnp.exp((ri + 1.0) * lg)
    to_end = jnp.exp((chunk - 1.0 - ri) * lg)
    g_chunk = jnp.exp(chunk * lg)
    gn_w = gn_ref[...]
    for sub in range(n_sub):
        sl = pl.ds(sub * chunk, chunk)
        cos = cos_ref[sl, :]
        sin = sin_ref[sl, :]
        q = _rope128(q_ref[sl, :], cos, sin)
        k = _rope128(k_ref[sl, :], cos, sin) * k_scale
        qb = q.astype(BF16)
        vb = v_ref[sl, :].astype(BF16)
        s = _dot_nt(qb, k.astype(BF16)) * decay
        st = state[...]
        o = _dot(s.astype(BF16), vb) + _dot(qb, st.astype(BF16)) * xi
        kw_t = jnp.transpose(k * to_end).astype(BF16)
        state[...] = g_chunk * st + _dot(kw_t, vb)
        o_ref[sl, :] = _group_norm_gate(o, g_ref[sl, :], gn_w)

    @pl.when(ci == pl.num_programs(2) - 1)
    def _():
        st_ref[0, 0] = state[...]


def _ret_prompt(z, cos128, sin128, lgam, gn_w, *, batch, seq, n_heads, dk, dv, rows, offs, total_rows):
    rq_off, rk_off, rv_off, rg_off = offs
    nr = seq // rows
    qk_spec = lambda off: pl.BlockSpec((rows, dk), lambda b, h, c: (b * nr + c, off // dk + h))
    v_spec = lambda off: pl.BlockSpec((rows, dv), lambda b, h, c: (b * nr + c, off // dv + h))
    tab = pl.BlockSpec((rows, LANES), lambda b, h, c: (b * nr + c, 0))
    return pl.pallas_call(
        functools.partial(_ret_prompt_kernel, n_sub=rows // RET_CHUNK, chunk=RET_CHUNK, k_scale=dk ** -0.5),
        grid=(batch, n_heads, nr),
        in_specs=[
            qk_spec(rq_off), qk_spec(rk_off), v_spec(rv_off), v_spec(rg_off), tab, tab,
            pl.BlockSpec((1, 1, LANES), lambda b, h, c: (h, 0, 0)),
            pl.BlockSpec((1, dv), lambda b, h, c: (0, h)),
        ],
        out_specs=[
            pl.BlockSpec((rows, dv), lambda b, h, c: (b * nr + c, h)),
            pl.BlockSpec((1, 1, dk, dv), lambda b, h, c: (b, h, 0, 0)),
        ],
        out_shape=[
            jax.ShapeDtypeStruct((total_rows, n_heads * dv), BF16),
            jax.ShapeDtypeStruct((batch, n_heads, dk, dv), F32),
        ],
        scratch_shapes=[pltpu.VMEM((dk, dv), F32)],
        compiler_params=_params("parallel", "parallel", "arbitrary"),
    )(z, z, z, z, cos128, sin128, lgam, gn_w)


def _ret_sample_kernel(q_ref, k_ref, v_ref, g_ref, cos_ref, sin_ref, lg_ref, gn_ref, st_in_ref, o_ref, st_out_ref,
                       *, nb, t_log2, k_scale):
    ts = 1 << t_log2
    rows = nb * ts
    lg = lg_ref[0][:, :1]
    ii = lax.broadcasted_iota(jnp.int32, (rows, rows), 0)
    jj = lax.broadcasted_iota(jnp.int32, (rows, rows), 1)
    diff = (ii - jj).astype(F32)
    keep = ((ii >> t_log2) == (jj >> t_log2)) & (ii >= jj)
    decay = jnp.where(keep, jnp.exp(jnp.maximum(diff, 0.0) * lg), 0.0)
    ri = lax.broadcasted_iota(jnp.int32, (rows, 1), 0)
    rb = ri >> t_log2
    ti = (ri & (ts - 1)).astype(F32)
    xi = jnp.exp((ti + 1.0) * lg)
    to_end = jnp.exp((ts - 1.0 - ti) * lg)
    g_chunk = jnp.exp(ts * lg)
    cos = cos_ref[...]
    sin = sin_ref[...]
    q = _rope128(q_ref[...], cos, sin)
    k = _rope128(k_ref[...], cos, sin) * k_scale
    qb = q.astype(BF16)
    vb = v_ref[...].astype(BF16)
    s = _dot_nt(qb, k.astype(BF16)) * decay
    o_in = _dot(s.astype(BF16), vb)
    kw = k * to_end
    o_cross = jnp.zeros(o_in.shape, F32)
    for b in range(nb):
        st = st_in_ref[b, 0]
        mine = rb == b
        o_cross = jnp.where(mine, _dot(qb, st.astype(BF16)), o_cross)
        kw_t = jnp.transpose(jnp.where(mine, kw, 0.0)).astype(BF16)
        st_out_ref[b, 0] = g_chunk * st + _dot(kw_t, vb)
    o_ref[...] = _group_norm_gate(o_in + o_cross * xi, g_ref[...], gn_ref[...])


def _ret_sample(z, cos128, sin128, lgam, gn_w, state, *, row0, dec_batch, dec_seq, n_heads, dk, dv, offs):
    rq_off, rk_off, rv_off, rg_off = offs
    t_log2 = dec_seq.bit_length() - 1
    assert 1 << t_log2 == dec_seq
    rows = LANES
    nb = rows // dec_seq
    nblk = dec_batch // nb
    rb0 = row0 // rows
    qk_spec = lambda off: pl.BlockSpec((rows, dk), lambda i, h: (rb0 + i, off // dk + h))
    v_spec = lambda off: pl.BlockSpec((rows, dv), lambda i, h: (rb0 + i, off // dv + h))
    tab = pl.BlockSpec((rows, LANES), lambda i, h: (rb0 + i, 0))
    return pl.pallas_call(
        functools.partial(_ret_sample_kernel, nb=nb, t_log2=t_log2, k_scale=dk ** -0.5),
        grid=(nblk, n_heads),
        in_specs=[
            qk_spec(rq_off), qk_spec(rk_off), v_spec(rv_off), v_spec(rg_off), tab, tab,
            pl.BlockSpec((1, 1, LANES), lambda i, h: (h, 0, 0)),
            pl.BlockSpec((1, dv), lambda i, h: (0, h)),
            pl.BlockSpec((nb, 1, dk, dv), lambda i, h: (i, h, 0, 0)),
        ],
        out_specs=[
            pl.BlockSpec((rows, dv), lambda i, h: (i, h)),
            pl.BlockSpec((nb, 1, dk, dv), lambda i, h: (i, h, 0, 0)),
        ],
        out_shape=[
            jax.ShapeDtypeStruct((dec_batch * dec_seq, n_heads * dv), BF16),
            jax.ShapeDtypeStruct(state.shape, F32),
        ],
        compiler_params=_params("parallel", "parallel"),
    )(z, z, z, z, cos128, sin128, lgam, gn_w, state)


def _softmax_rows(s):
    e = jnp.exp(s - jnp.max(s, axis=-1, keepdims=True))
    return e / jnp.sum(e, axis=-1, keepdims=True)


def _mem_attn_prompt_kernel(q_ref, k_ref, v_ref, o_ref, *, scale):
    s = _dot_nt(q_ref[...].astype(BF16), k_ref[...].astype(BF16)) * scale
    o_ref[...] = _dot(_softmax_rows(s).astype(BF16), v_ref[...].astype(BF16)).astype(BF16)


def _mem_attn_prompt(z, kv, *, batch, seq, mem_tokens, n_heads, hd, tq, mq_off, total_rows):
    nq = seq // tq
    return pl.pallas_call(
        functools.partial(_mem_attn_prompt_kernel, scale=hd ** -0.5),
        grid=(batch, n_heads, nq),
        in_specs=[
            pl.BlockSpec((tq, hd), lambda b, h, i: (b * nq + i, mq_off // hd + h)),
            pl.BlockSpec((mem_tokens, hd), lambda b, h, i: (b, h)),
            pl.BlockSpec((mem_tokens, hd), lambda b, h, i: (b, n_heads + h)),
        ],
        out_specs=pl.BlockSpec((tq, hd), lambda b, h, i: (b * nq + i, h)),
        out_shape=jax.ShapeDtypeStruct((total_rows, n_heads * hd), BF16),
        compiler_params=_params("parallel", "parallel", "parallel"),
    )(z, kv, kv)


def _mem_attn_sample_kernel(q_ref, k_ref, v_ref, o_ref, *, nb, t_log2, scale):
    q = q_ref[...].astype(BF16)
    rb = lax.broadcasted_iota(jnp.int32, (q.shape[0], 1), 0) >> t_log2
    o = jnp.zeros(o_ref.shape, F32)
    for b in range(nb):
        s = _dot_nt(q, k_ref[b].astype(BF16)) * scale
        ob = _dot(_softmax_rows(s).astype(BF16), v_ref[b].astype(BF16))
        o = jnp.where(rb == b, ob, o)
    o_ref[...] = o.astype(BF16)


def _mem_attn_sample(z, mem_k, mem_v, *, row0, dec_batch, dec_seq, n_heads, hd, nb, mq_off):
    t_log2 = dec_seq.bit_length() - 1
    rows = nb * dec_seq
    rb0 = row0 // rows
    mem_tokens = mem_k.shape[1]
    return pl.pallas_call(
        functools.partial(_mem_attn_sample_kernel, nb=nb, t_log2=t_log2, scale=hd ** -0.5),
        grid=(dec_batch // nb, n_heads),
        in_specs=[
            pl.BlockSpec((rows, hd), lambda i, h: (rb0 + i, mq_off // hd + h)),
            pl.BlockSpec((nb, mem_tokens, hd), lambda i, h: (i, 0, h)),
            pl.BlockSpec((nb, mem_tokens, hd), lambda i, h: (i, 0, h)),
        ],
        out_specs=pl.BlockSpec((rows, hd), lambda i, h: (i, h)),
        out_shape=jax.ShapeDtypeStruct((dec_batch * dec_seq, n_heads * hd), BF16),
        compiler_params=_params("parallel", "parallel"),
    )(z, mem_k, mem_v)


def _merge_kernel(oap_ref, obp_ref, ocp_ref, oas_ref, obs_ref, ocs_ref, wa_ref, wb_ref, wc_ref, ga_ref, gb_ref,
                  gc_ref, m_ref, *, n_first):
    def run(oa_ref, ob_ref, oc_ref):
        merged = jax.nn.sigmoid(ga_ref[...]) * _dot(oa_ref[...], wa_ref[...])
        merged = merged + jax.nn.sigmoid(gb_ref[...]) * _dot(ob_ref[...], wb_ref[...])
        merged = merged + jax.nn.sigmoid(gc_ref[...]) * _dot(oc_ref[...], wc_ref[...])
        m_ref[...] = merged.astype(BF16)

    i = pl.program_id(0)

    @pl.when(i < n_first)
    def _():
        run(oap_ref, obp_ref, ocp_ref)

    @pl.when(i >= n_first)
    def _():
        run(oas_ref, obs_ref, ocs_ref)


def _merge(prompt_outs, sample_outs, w_a, w_b, w_c, z, *, tm, tn, gz_off):
    tp = prompt_outs[0].shape[0]
    t = tp + sample_outs[0].shape[0]
    n_first = tp // tm
    d = w_a.shape[1]
    nj = d // tn
    act_p = lambda a: pl.BlockSpec((tm, a.shape[1]), lambda i, j: (jnp.minimum(i, n_first - 1), 0))
    act_s = lambda a: pl.BlockSpec((tm, a.shape[1]), lambda i, j: (jnp.maximum(i - n_first, 0), 0))
    wsp = lambda w: pl.BlockSpec((w.shape[0], tn), lambda i, j: (0, j))
    gate = lambda br: pl.BlockSpec((tm, tn), lambda i, j: (i, (gz_off + br * d) // tn + j))
    return pl.pallas_call(
        functools.partial(_merge_kernel, n_first=n_first),
        grid=(t // tm, nj),
        in_specs=[act_p(a) for a in prompt_outs] + [act_s(a) for a in sample_outs]
        + [wsp(w_a), wsp(w_b), wsp(w_c), gate(0), gate(1), gate(2)],
        out_specs=pl.BlockSpec((tm, tn), lambda i, j: (i, j)),
        out_shape=jax.ShapeDtypeStruct((t, d), BF16),
        compiler_params=_params("parallel", "parallel"),
    )(*prompt_outs, *sample_outs, w_a, w_b, w_c, z, z, z)


def _out_proj_kernel(x_ref, m_ref, wo_ref, nw_ref, wrh_ref, wrl_ref, br_ref, x1_ref, h_ref, lg_ref):
    x1 = x_ref[...] + _dot(m_ref[...], wo_ref[...])
    x1_ref[...] = x1
    h = _rms(x1, nw_ref[...])
    h_ref[...] = h
    h_hi = h.astype(BF16)
    h_lo = (h - h_hi.astype(F32)).astype(BF16)
    w_hi = wrh_ref[...]
    lg_ref[...] = _dot(h_hi, w_hi) + _dot(h_lo, w_hi) + _dot(h_hi, wrl_ref[...]) + br_ref[...]


def _out_proj(x, merged, w_o, norm_w, w_router_hi, w_router_lo, b_router, *, tm):
    t, d = x.shape
    row = lambda i: (i, 0)
    const = lambda i: (0, 0)
    return pl.pallas_call(
        _out_proj_kernel,
        grid=(t // tm,),
        in_specs=[
            pl.BlockSpec((tm, d), row),
            pl.BlockSpec((tm, d), row),
            pl.BlockSpec((d, d), const),
            pl.BlockSpec((1, d), const),
            pl.BlockSpec((d, LANES), const),
            pl.BlockSpec((d, LANES), const),
            pl.BlockSpec((1, LANES), const),
        ],
        out_specs=[pl.BlockSpec((tm, d), row), pl.BlockSpec((tm, d), row), pl.BlockSpec((tm, LANES), row)],
        out_shape=[
            jax.ShapeDtypeStruct((t, d), F32),
            jax.ShapeDtypeStruct((t, d), F32),
            jax.ShapeDtypeStruct((t, LANES), F32),
        ],
        compiler_params=_params("parallel"),
    )(x, merged, w_o, norm_w, w_router_hi, w_router_lo, b_router)


def _gather_rows(src_hbm, idx_ref, base, buf, sem, n_rows):
    def issue(r, carry):
        pltpu.make_async_copy(src_hbm.at[pl.ds(idx_ref[base + r], 1), :], buf.at[pl.ds(r, 1), :], sem).start()
        return carry

    lax.fori_loop(0, n_rows, issue, 0, unroll=8)


def _wait_rows(src_hbm, buf, sem, n_rows):
    pltpu.make_async_copy(src_hbm.at[pl.ds(0, n_rows), :], buf, sem).wait()


def _experts_kernel(be_ref, nu_ref, tok_ref, h_hbm, wg_ref, wu_ref, wd_ref, y_ref, buf, wg_b, wu_b, wd_b, sems,
                    *, blk):
    i = pl.program_id(0)
    n_used = nu_ref[0]
    slot = i % 2

    @pl.when(i == 0)
    def _():
        _gather_rows(h_hbm, tok_ref, 0, buf.at[0], sems.at[0], blk)

    @pl.when(i + 1 < n_used)
    def _():
        _gather_rows(h_hbm, tok_ref, (i + 1) * blk, buf.at[1 - slot], sems.at[1 - slot], blk)

    @pl.when(i < n_used)
    def _():
        @pl.when((i == 0) | (be_ref[i] != be_ref[jnp.maximum(i - 1, 0)]))
        def _():
            wg_b[...] = wg_ref[...].astype(BF16)
            wu_b[...] = wu_ref[...].astype(BF16)
            wd_b[...] = wd_ref[...].astype(BF16)

        _wait_rows(h_hbm, buf.at[slot], sems.at[slot], blk)
        xb = buf[slot].astype(BF16)
        g = _dot(xb, wg_b[...])
        u = _dot(xb, wu_b[...])
        a = (g * jax.nn.sigmoid(g) * u).astype(BF16)
        y_ref[...] = _dot(a, wd_b[...])

    @pl.when(i >= n_used)
    def _():
        y_ref[...] = jnp.zeros(y_ref.shape, F32)


def _experts(blk_expert, n_used, row_tok, h, w_gate, w_up, w_down, *, n_blk, blk):
    d = h.shape[1]
    de = w_gate.shape[2]
    grid_spec = pltpu.PrefetchScalarGridSpec(
        num_scalar_prefetch=3,
        grid=(n_blk,),
        in_specs=[
            pl.BlockSpec(memory_space=pl.ANY),
            pl.BlockSpec((None, d, de), lambda i, be, nu, tok: (be[i], 0, 0)),
            pl.BlockSpec((None, d, de), lambda i, be, nu, tok: (be[i], 0, 0)),
            pl.BlockSpec((None, de, d), lambda i, be, nu, tok: (be[i], 0, 0)),
        ],
        out_specs=pl.BlockSpec((blk, d), lambda i, be, nu, tok: (i, 0)),
        scratch_shapes=[
            pltpu.VMEM((2, blk, d), F32),
            pltpu.VMEM((d, de), BF16),
            pltpu.VMEM((d, de), BF16),
            pltpu.VMEM((de, d), BF16),
            pltpu.SemaphoreType.DMA((2,)),
        ],
    )
    return pl.pallas_call(
        functools.partial(_experts_kernel, blk=blk),
        grid_spec=grid_spec,
        out_shape=jax.ShapeDtypeStruct((n_blk * blk, d), F32),
        compiler_params=_params("arbitrary"),
    )(blk_expert, n_used, row_tok, h, w_gate, w_up, w_down)


def _combine_kernel(d0_ref, d1_ref, y_hbm, x1_ref, g0_ref, g1_ref, nw_ref, op_ref, os_ref, buf0, buf1, sems,
                    *, tm, n_first):
    i = pl.program_id(0)
    slot = i % 2

    def issue(step, s):
        _gather_rows(y_hbm, d0_ref, step * tm, buf0.at[s], sems.at[0, s], tm)
        _gather_rows(y_hbm, d1_ref, step * tm, buf1.at[s], sems.at[1, s], tm)

    @pl.when(i == 0)
    def _():
        issue(0, 0)

    @pl.when(i + 1 < pl.num_programs(0))
    def _():
        issue(i + 1, 1 - slot)

    _wait_rows(y_hbm, buf0.at[slot], sems.at[0, slot], tm)
    _wait_rows(y_hbm, buf1.at[slot], sems.at[1, slot], tm)
    reps = x1_ref.shape[1] // LANES
    g0 = jnp.tile(g0_ref[...], (1, reps))
    g1 = jnp.tile(g1_ref[...], (1, reps))
    x2 = x1_ref[...] + (buf0[slot] * g0 + buf1[slot] * g1)
    y = _rms(x2, nw_ref[...])

    @pl.when(i < n_first)
    def _():
        op_ref[...] = y

    @pl.when(i >= n_first)
    def _():
        os_ref[...] = y


def _combine(d0, d1, y_rows, x1, g0, g1, norm_w, *, tm, rows_first):
    t, d = x1.shape
    n_first = rows_first // tm
    row = lambda i, a, b: (i, 0)
    grid_spec = pltpu.PrefetchScalarGridSpec(
        num_scalar_prefetch=2,
        grid=(t // tm,),
        in_specs=[
            pl.BlockSpec(memory_space=pl.ANY),
            pl.BlockSpec((tm, d), row),
            pl.BlockSpec((tm, LANES), row),
            pl.BlockSpec((tm, LANES), row),
            pl.BlockSpec((1, d), lambda i, a, b: (0, 0)),
        ],
        out_specs=[
            pl.BlockSpec((tm, d), lambda i, a, b: (jnp.minimum(i, n_first - 1), 0)),
            pl.BlockSpec((tm, d), lambda i, a, b: (jnp.maximum(i - n_first, 0), 0)),
        ],
        scratch_shapes=[
            pltpu.VMEM((2, tm, d), F32),
            pltpu.VMEM((2, tm, d), F32),
            pltpu.SemaphoreType.DMA((2, 2)),
        ],
    )
    return pl.pallas_call(
        functools.partial(_combine_kernel, tm=tm, n_first=n_first),
        grid_spec=grid_spec,
        out_shape=[jax.ShapeDtypeStruct((rows_first, d), F32), jax.ShapeDtypeStruct((t - rows_first, d), F32)],
        compiler_params=_params("arbitrary"),
    )(d0, d1, y_rows, x1, g0, g1, norm_w)


def _route(logits, n_groups, per_group, top_k, blk):
    t = logits.shape[0]
    n_exp = n_groups * per_group
    assert top_k == 2
    grp_p = jax.nn.softmax(logits[:, :n_groups], axis=-1)
    g_idx = jnp.argmax(grp_p, axis=-1, keepdims=True)
    g_prob = jnp.max(grp_p, axis=-1, keepdims=True)
    e_logit = logits[:, n_groups:n_groups + n_exp].reshape(t, n_groups, per_group)
    in_grp = jnp.take_along_axis(e_logit, g_idx[:, :, None], axis=1)[:, 0, :]
    i1 = jnp.argmax(in_grp, axis=-1, keepdims=True)
    rest = jnp.where(jnp.arange(per_group)[None, :] == i1, -jnp.inf, in_grp)
    i2 = jnp.argmax(rest, axis=-1, keepdims=True)
    top_i = jnp.concatenate([i1, i2], axis=-1)
    top_v = jnp.concatenate([jnp.max(in_grp, axis=-1, keepdims=True), jnp.max(rest, axis=-1, keepdims=True)], axis=-1)
    gate = g_prob * jax.nn.softmax(top_v, axis=-1)
    expert = (g_idx * per_group + top_i).reshape(-1).astype(jnp.int32)
    a = t * top_k
    onehot = expert[:, None] == jnp.arange(n_exp, dtype=jnp.int32)[None, :]
    csum = jnp.cumsum(onehot.astype(jnp.int32), axis=0)
    counts = csum[-1]
    rank = jnp.sum(jnp.where(onehot, csum - 1, 0), axis=1)
    padded = (counts + blk - 1) // blk * blk
    pad_end = jnp.cumsum(padded)
    pad_start = pad_end - padded
    dest = (pad_start[expert] + rank).astype(jnp.int32)
    n_blk = (a + n_exp * (blk - 1) + blk - 1) // blk
    tok = jnp.arange(a, dtype=jnp.int32) // top_k
    row_tok = jnp.zeros((n_blk * blk,), jnp.int32).at[dest].set(tok)
    blk_start = jnp.arange(n_blk, dtype=jnp.int32) * blk
    blk_expert = jnp.minimum(jnp.sum(pad_end[None, :] <= blk_start[:, None], axis=1), n_exp - 1).astype(jnp.int32)
    n_used = (pad_end[-1:] // blk).astype(jnp.int32)
    return gate, dest.reshape(t, top_k), row_tok, blk_expert, n_used, n_blk


def _pick(n, *prefs):
    for p in prefs:
        if n % p == 0:
            return p
    return n


def kernel(x_prompt, x_sample, mem_prompt, cache_ckv, cache_krope, state_ret, cache_mem_k, cache_mem_v, page_table, attn_norm_w, w_in, mla_q_norm_w, mla_w_uq, mla_kv_norm_w, mla_w_uk, mla_w_uv, ret_gn_w, mem_norm_w, mem_w_kv, w_branch_a, w_branch_b, w_branch_c, w_out, ffn_norm_w, router_grp_w, router_grp_b, router_exp_w, router_exp_b, exp_w_gate, exp_w_up, exp_w_down, final_norm_w):
    batch, seq, d = x_prompt.shape
    dec_batch, dec_seq, _ = x_sample.shape
    depth = w_in.shape[0]
    q_rank, n_heads, qh = mla_w_uq.shape[1:]
    kv_rank, _, nope = mla_w_uk.shape[1:]
    rope_dim = qh - nope
    v_dim = mla_w_uv.shape[3]
    ret_heads, ret_dv = ret_gn_w.shape[1:]
    ret_dk = state_ret.shape[3]
    mem_tokens = mem_prompt.shape[1]
    mem_heads, mem_hd = cache_mem_k.shape[3:]
    mem_width = mem_heads * mem_hd
    n_groups = router_grp_w.shape[2]
    n_exp = router_exp_w.shape[2]
    per_group = n_exp // n_groups
    top_k = 2
    page = cache_ckv.shape[2]
    past_len = page_table.shape[1] * page
    assert nope == LANES and v_dim == LANES and rope_dim == 64 and ret_dk == LANES
    assert q_rank == kv_rank and n_groups + n_exp <= LANES

    tp = batch * seq
    ts = dec_batch * dec_seq
    t = tp + ts
    ret_qk = ret_heads * ret_dk
    ret_v = ret_heads * ret_dv
    cq_off, ckv_off = 0, q_rank
    rq_off = ckv_off + kv_rank
    rk_off = rq_off + ret_qk
    rv_off = rk_off + ret_qk
    rg_off = rv_off + ret_v
    mq_off = rg_off + ret_v
    gz_off = mq_off + mem_width
    ret_offs = (rq_off, rk_off, rv_off, rg_off)

    pos = jnp.concatenate([jnp.tile(jnp.arange(seq), batch), jnp.tile(past_len + jnp.arange(dec_seq), dec_batch)])
    cos64, sin64 = _rope_tables(pos, rope_dim)
    cos128, sin128 = _rope_tables(pos, ret_dk)
    lgam = jnp.log1p(-jnp.exp2(-5.0 - jnp.arange(ret_heads, dtype=F32)))
    lgam = jnp.broadcast_to(lgam[:, None, None], (ret_heads, 1, LANES))

    xp = x_prompt.reshape(tp, d)
    xs = x_sample.reshape(ts, d)
    x = jnp.concatenate([xp, xs], axis=0)

    assert depth == 1, "the fused final norm assumes a single layer"
    l = 0
    tm_mid = _pick(t, 256)

    wi = w_in[l]
    kr_lo = q_rank + kv_rank
    w_main = jnp.concatenate([wi[:, :kr_lo], wi[:, kr_lo + rope_dim:]], axis=1).astype(BF16)
    w_kr = jnp.pad(wi[:, kr_lo:kr_lo + rope_dim], ((0, 0), (0, LANES - rope_dim))).astype(BF16)
    z, zk = _norm_proj(x, attn_norm_w[l], w_main, w_kr, tm=_pick(t, 1536, 1024, 512),
                       tn=_pick(w_main.shape[1], 512))

    wuq = mla_w_uq[l]
    wqn = wuq[:, :, :nope].reshape(q_rank, n_heads * LANES).astype(BF16)
    wqr = jnp.pad(wuq[:, :, nope:], ((0, 0), (0, 0), (0, LANES - rope_dim))).reshape(q_rank, n_heads * LANES).astype(BF16)
    wuk = mla_w_uk[l].reshape(kv_rank, n_heads * nope).astype(BF16)
    wuv = mla_w_uv[l].reshape(kv_rank, n_heads * v_dim).astype(BF16)
    qcat, c, kr, kcat, v = _mla_prep(
        z, zk, cos64, sin64, mla_q_norm_w[l].reshape(1, -1), mla_kv_norm_w[l].reshape(1, -1), wqn, wqr, wuk, wuv,
        tm=tm_mid, n_heads=n_heads, q_rank=q_rank, kv_rank=kv_rank, rope_dim=rope_dim)

    scale = qh ** -0.5
    oa_p = _mla_prompt(qcat, kcat, v, batch=batch, seq=seq, n_heads=n_heads, tq=_pick(seq, 512), scale=scale,
                       total_rows=tp)
    wukt = jnp.transpose(mla_w_uk[l], (1, 2, 0)).astype(BF16)
    qlat, qr = _absorb_q(qcat, wukt, ts=ts, row_block=tp // ts, n_heads=n_heads, kv_rank=kv_rank)
    rows = dec_seq * n_heads
    new_pad = LANES - dec_seq
    cnew = jnp.pad(c[tp:].reshape(dec_batch, dec_seq, kv_rank), ((0, 0), (0, new_pad), (0, 0))).astype(BF16)
    krnew = jnp.pad(kr[tp:].reshape(dec_batch, dec_seq, rope_dim), ((0, 0), (0, new_pad), (0, 0))).astype(BF16)
    o_lat = _mla_decode(
        page_table, qlat.reshape(dec_batch, rows, kv_rank), qr.reshape(dec_batch, rows, LANES), cnew, krnew,
        cache_ckv[l], jnp.swapaxes(cache_krope[l], 1, 2), npg=_pick(page_table.shape[1], 32), scale=scale,
        n_heads=n_heads, rope_dim=rope_dim)
    wuv_h = jnp.transpose(mla_w_uv[l], (1, 0, 2)).astype(BF16)
    oa_s = _head_proj(o_lat.reshape(ts, n_heads * kv_rank), wuv_h, n_heads=n_heads)

    gn_w = ret_gn_w[l].reshape(1, ret_v)
    ob_p, st_p = _ret_prompt(z, cos128, sin128, lgam, gn_w, batch=batch, seq=seq, n_heads=ret_heads, dk=ret_dk,
                             dv=ret_dv, rows=_pick(seq, 512), offs=ret_offs, total_rows=tp)
    ob_s, st_s = _ret_sample(z, cos128, sin128, lgam, gn_w, state_ret[l], row0=tp, dec_batch=dec_batch,
                             dec_seq=dec_seq, n_heads=ret_heads, dk=ret_dk, dv=ret_dv, offs=ret_offs)

    mem_rows = batch * mem_tokens
    (kv,) = _norm_proj(mem_prompt.reshape(mem_rows, d), mem_norm_w[l], mem_w_kv[l].astype(BF16),
                       tm=_pick(mem_rows, 1024, 512), tn=_pick(2 * mem_width, 512))
    oc_p = _mem_attn_prompt(z, kv, batch=batch, seq=seq, mem_tokens=mem_tokens, n_heads=mem_heads, hd=mem_hd,
                            tq=_pick(seq, 512), mq_off=mq_off, total_rows=tp)
    oc_s = _mem_attn_sample(z, cache_mem_k[l].reshape(dec_batch, mem_tokens, mem_width),
                            cache_mem_v[l].reshape(dec_batch, mem_tokens, mem_width), row0=tp,
                            dec_batch=dec_batch, dec_seq=dec_seq, n_heads=mem_heads, hd=mem_hd, nb=8, mq_off=mq_off)

    tm_merge = _pick(ts, 512, 256, 128)
    assert tp % tm_merge == 0
    merged = _merge((oa_p, ob_p, oc_p), (oa_s, ob_s, oc_s), w_branch_a[l].astype(BF16), w_branch_b[l].astype(BF16),
                    w_branch_c[l].astype(BF16), z, tm=tm_merge, tn=_pick(d, 512), gz_off=gz_off)
    n_route = n_groups + n_exp
    w_router = jnp.pad(jnp.concatenate([router_grp_w[l], router_exp_w[l]], axis=1), ((0, 0), (0, LANES - n_route)))
    w_router_hi = w_router.astype(BF16)
    w_router_lo = (w_router - w_router_hi.astype(F32)).astype(BF16)
    b_router = jnp.pad(jnp.concatenate([router_grp_b[l], router_exp_b[l]]), (0, LANES - n_route)).reshape(1, LANES)
    x1, h2, logits = _out_proj(x, merged, w_out[l].astype(BF16), ffn_norm_w[l].reshape(1, d), w_router_hi,
                               w_router_lo, b_router, tm=tm_mid)

    gate, dest, row_tok, blk_expert, n_used, n_blk = _route(logits, n_groups, per_group, top_k, MOE_BLOCK)
    y_rows = _experts(blk_expert, n_used, row_tok, h2, exp_w_gate[l], exp_w_up[l], exp_w_down[l], n_blk=n_blk,
                      blk=MOE_BLOCK)
    g0 = jnp.broadcast_to(gate[:, 0:1], (t, LANES))
    g1 = jnp.broadcast_to(gate[:, 1:2], (t, LANES))
    tm_c = _pick(ts, 128)
    assert tp % tm_c == 0
    y_p, y_s = _combine(dest[:, 0], dest[:, 1], y_rows, x1, g0, g1, final_norm_w.reshape(1, d), tm=tm_c, rows_first=tp)

    lead = lambda a: a[None]
    return (
        y_p.reshape(batch, seq, d),
        y_s.reshape(dec_batch, dec_seq, d),
        lead(c[:tp].reshape(batch, seq, kv_rank)),
        lead(kr[:tp].reshape(batch, seq, rope_dim)),
        lead(st_p),
        lead(kv[:, :mem_width].reshape(batch, mem_tokens, mem_heads, mem_hd)),
        lead(kv[:, mem_width:].reshape(batch, mem_tokens, mem_heads, mem_hd)),
        lead(c[tp:].reshape(dec_batch, dec_seq, kv_rank)),
        lead(kr[tp:].reshape(dec_batch, dec_seq, rope_dim)),
        lead(st_s),
    )
```

```python
import functools

import jax
import jax.numpy as jnp
from jax import lax
from jax.experimental import pallas as pl
from jax.experimental.pallas import tpu as pltpu

F32 = jnp.float32
BF16 = jnp.bfloat16

NORM_EPS = 1e-6
GN_EPS = 1e-5
ROPE_THETA = 10000.0
RET_CHUNK = 128
MOE_BLOCK = 128
DECODE_SLOTS = 3
LANES = 128
NEG_BIG = -1e30
VMEM_LIMIT_BYTES = 56 * 1024 * 1024


def _params(*sem):
    return pltpu.CompilerParams(dimension_semantics=sem, vmem_limit_bytes=VMEM_LIMIT_BYTES)


def _dot(a, b):
    return jnp.dot(a, b, preferred_element_type=F32)


def _dot_nt(a, b):
    return lax.dot_general(a, b, (((1,), (1,)), ((), ())), preferred_element_type=F32)


def _rms(x, w):
    return x * lax.rsqrt(jnp.mean(x * x, axis=-1, keepdims=True) + NORM_EPS) * w


def _rope64(x, cos, sin):
    lane = lax.broadcasted_iota(jnp.int32, x.shape, 1)
    first = (lane & 63) < 32
    swapped = jnp.where(first, pltpu.roll(x, 96, 1), pltpu.roll(x, 32, 1))
    return x * cos + swapped * sin


def _rope128(x, cos, sin):
    return x * cos + pltpu.roll(x, 64, 1) * sin


def _rope_tables(pos, dim):
    half = dim // 2
    inv = ROPE_THETA ** (-jnp.arange(half, dtype=F32) / half)
    ang = pos.astype(F32)[:, None] * inv[None, :]
    cos = jnp.cos(ang)
    sin = jnp.sin(ang)
    c = jnp.concatenate([cos, cos], axis=-1)
    s = jnp.concatenate([-sin, sin], axis=-1)
    reps = LANES // dim
    return jnp.tile(c, (1, reps)), jnp.tile(s, (1, reps))


def _norm_proj_kernel(x_ref, nw_ref, w_ref, z_ref, h_ref):
    @pl.when(pl.program_id(1) == 0)
    def _():
        h_ref[...] = _rms(x_ref[...], nw_ref[...]).astype(BF16)

    z_ref[...] = _dot(h_ref[...], w_ref[...])


def _norm_proj_extra_kernel(x_ref, nw_ref, w_ref, we_ref, z_ref, ze_ref, h_ref):
    @pl.when(pl.program_id(1) == 0)
    def _():
        h = _rms(x_ref[...], nw_ref[...]).astype(BF16)
        h_ref[...] = h
        ze_ref[...] = _dot(h, we_ref[...])

    z_ref[...] = _dot(h_ref[...], w_ref[...])


def _norm_proj(x, norm_w, w, w_extra=None, *, tm, tn):
    t, d = x.shape
    n = w.shape[1]
    grid = (t // tm, n // tn)
    in_specs = [
        pl.BlockSpec((tm, d), lambda i, j: (i, 0)),
        pl.BlockSpec((1, d), lambda i, j: (0, 0)),
        pl.BlockSpec((d, tn), lambda i, j: (0, j)),
    ]
    out_specs = [pl.BlockSpec((tm, tn), lambda i, j: (i, j))]
    out_shape = [jax.ShapeDtypeStruct((t, n), F32)]
    args = [x, norm_w.reshape(1, d), w]
    body = _norm_proj_kernel
    if w_extra is not None:
        ne = w_extra.shape[1]
        in_specs.append(pl.BlockSpec((d, ne), lambda i, j: (0, 0)))
        out_specs.append(pl.BlockSpec((tm, ne), lambda i, j: (i, 0)))
        out_shape.append(jax.ShapeDtypeStruct((t, ne), F32))
        args.append(w_extra)
        body = _norm_proj_extra_kernel
    return pl.pallas_call(
        body,
        grid=grid,
        in_specs=in_specs,
        out_specs=out_specs,
        out_shape=out_shape,
        scratch_shapes=[pltpu.VMEM((tm, d), BF16)],
        compiler_params=_params("parallel", "arbitrary"),
    )(*args)


def _mla_prep_kernel(cq_ref, ckv_ref, zk_ref, cos_ref, sin_ref, qnw_ref, kvnw_ref, wqn_ref, wqr_ref,
                     wuk_ref, wuv_ref, qcat_ref, c_ref, kr_ref, kcat_ref, v_ref, *, n_heads, rope_dim):
    cos = cos_ref[...]
    sin = sin_ref[...]
    cqn = _rms(cq_ref[...], qnw_ref[...]).astype(BF16)
    qn = _dot(cqn, wqn_ref[...])
    qr = _dot(cqn, wqr_ref[...])
    c = _rms(ckv_ref[...], kvnw_ref[...])
    c_ref[...] = c
    cb = c.astype(BF16)
    kn = _dot(cb, wuk_ref[...])
    v_ref[...] = _dot(cb, wuv_ref[...]).astype(BF16)
    kr = _rope64(zk_ref[...], cos, sin)
    kr_ref[...] = kr[:, :rope_dim]
    krb = kr.astype(BF16)
    for h in range(n_heads):
        lo = h * 2 * LANES
        sl = slice(h * LANES, (h + 1) * LANES)
        qcat_ref[:, lo:lo + LANES] = qn[:, sl].astype(BF16)
        qcat_ref[:, lo + LANES:lo + 2 * LANES] = _rope64(qr[:, sl], cos, sin).astype(BF16)
        kcat_ref[:, lo:lo + LANES] = kn[:, sl].astype(BF16)
        kcat_ref[:, lo + LANES:lo + 2 * LANES] = krb


def _mla_prep(z, zk, cos64, sin64, qnw, kvnw, wqn, wqr, wuk, wuv, *, tm, n_heads, q_rank, kv_rank, rope_dim):
    t = z.shape[0]
    row = lambda i: (i, 0)
    const = lambda i: (0, 0)
    hw = n_heads * LANES
    return pl.pallas_call(
        functools.partial(_mla_prep_kernel, n_heads=n_heads, rope_dim=rope_dim),
        grid=(t // tm,),
        in_specs=[
            pl.BlockSpec((tm, q_rank), lambda i: (i, 0)),
            pl.BlockSpec((tm, kv_rank), lambda i: (i, q_rank // kv_rank)),
            pl.BlockSpec((tm, LANES), row),
            pl.BlockSpec((tm, LANES), row),
            pl.BlockSpec((tm, LANES), row),
            pl.BlockSpec((1, q_rank), const),
            pl.BlockSpec((1, kv_rank), const),
            pl.BlockSpec((q_rank, hw), const),
            pl.BlockSpec((q_rank, hw), const),
            pl.BlockSpec((kv_rank, hw), const),
            pl.BlockSpec((kv_rank, hw), const),
        ],
        out_specs=[
            pl.BlockSpec((tm, 2 * hw), row),
            pl.BlockSpec((tm, kv_rank), row),
            pl.BlockSpec((tm, rope_dim), row),
            pl.BlockSpec((tm, 2 * hw), row),
            pl.BlockSpec((tm, hw), row),
        ],
        out_shape=[
            jax.ShapeDtypeStruct((t, 2 * hw), BF16),
            jax.ShapeDtypeStruct((t, kv_rank), F32),
            jax.ShapeDtypeStruct((t, rope_dim), F32),
            jax.ShapeDtypeStruct((t, 2 * hw), BF16),
            jax.ShapeDtypeStruct((t, hw), BF16),
        ],
        compiler_params=_params("parallel"),
    )(z, z, zk, cos64, sin64, qnw, kvnw, wqn, wqr, wuk, wuv)


def _flash_kernel(q_ref, k_ref, v_ref, o_ref, *, tq, scale):
    qi = pl.program_id(2)
    q = q_ref[...]

    def block(j, carry, diagonal):
        m, l, acc = carry
        start = pl.multiple_of(j * tq, tq)
        k = k_ref[pl.ds(start, tq), :]
        v = v_ref[pl.ds(start, tq), :]
        s = _dot_nt(q, k) * scale
        if diagonal:
            rows = lax.broadcasted_iota(jnp.int32, (tq, tq), 0)
            cols = lax.broadcasted_iota(jnp.int32, (tq, tq), 1)
            s = jnp.where(cols <= rows, s, NEG_BIG)
        m_new = jnp.maximum(m, jnp.max(s, axis=-1, keepdims=True))
        alpha = jnp.exp(m - m_new)
        p = jnp.exp(s - m_new)
        l = alpha * l + jnp.sum(p, axis=-1, keepdims=True)
        acc = alpha * acc + _dot(p.astype(BF16), v)
        return m_new, l, acc

    dv = v_ref.shape[-1]
    init = (jnp.full((tq, 1), NEG_BIG, F32), jnp.zeros((tq, 1), F32), jnp.zeros((tq, dv), F32))
    carry = lax.fori_loop(0, qi, functools.partial(block, diagonal=False), init)
    _, l, acc = block(qi, carry, True)
    o_ref[...] = (acc / l).astype(BF16)


def _mla_prompt(qcat, kcat, v, *, batch, seq, n_heads, tq, scale, total_rows):
    nq = seq // tq
    dv = v.shape[1] // n_heads
    return pl.pallas_call(
        functools.partial(_flash_kernel, tq=tq, scale=scale),
        grid=(batch, n_heads, nq),
        in_specs=[
            pl.BlockSpec((tq, 2 * LANES), lambda b, h, i: (b * nq + i, h)),
            pl.BlockSpec((seq, 2 * LANES), lambda b, h, i: (b, h)),
            pl.BlockSpec((seq, dv), lambda b, h, i: (b, h)),
        ],
        out_specs=pl.BlockSpec((tq, dv), lambda b, h, i: (b * nq + i, h)),
        out_shape=jax.ShapeDtypeStruct((total_rows, n_heads * dv), BF16),
        compiler_params=_params("parallel", "parallel", "arbitrary"),
    )(qcat, kcat, v)


def _absorb_q_kernel(qcat_ref, wukt_ref, qlat_ref, qr_ref, *, n_heads, kv_rank):
    for h in range(n_heads):
        lo = h * 2 * LANES
        qlat_ref[:, h * kv_rank:(h + 1) * kv_rank] = _dot(qcat_ref[:, lo:lo + LANES], wukt_ref[h]).astype(BF16)
        qr_ref[:, h * LANES:(h + 1) * LANES] = qcat_ref[:, lo + LANES:lo + 2 * LANES]


def _absorb_q(qcat, wukt, *, ts, row_block, n_heads, kv_rank):
    return pl.pallas_call(
        functools.partial(_absorb_q_kernel, n_heads=n_heads, kv_rank=kv_rank),
        grid=(1,),
        in_specs=[
            pl.BlockSpec((ts, qcat.shape[1]), lambda i: (row_block, 0)),
            pl.BlockSpec(wukt.shape, lambda i: (0, 0, 0)),
        ],
        out_specs=[
            pl.BlockSpec((ts, n_heads * kv_rank), lambda i: (0, 0)),
            pl.BlockSpec((ts, n_heads * LANES), lambda i: (0, 0)),
        ],
        out_shape=[
            jax.ShapeDtypeStruct((ts, n_heads * kv_rank), BF16),
            jax.ShapeDtypeStruct((ts, n_heads * LANES), BF16),
        ],
        compiler_params=_params("arbitrary"),
    )(qcat, wukt)


def _decode_kernel(pt_ref, qlat_ref, qr_ref, cnew_ref, krnew_ref, ckv_hbm, krt_hbm, o_ref, cbuf, kbuf, cb_ref, kb_ref,
                   m_ref, l_ref, acc_ref, sems, *, npg, scale, n_heads, rope_dim):
    b = pl.program_id(0)
    step = pl.program_id(1)
    steps = pl.num_programs(1)
    total = pl.num_programs(0) * steps
    n_slots = cbuf.shape[0]
    g_step = b * steps + step
    slot = g_step % n_slots
    page = cbuf.shape[2]

    def fetch(g):
        bb = g // steps
        ss = g % steps
        sl = g % n_slots

        def body(j, carry):
            pg = pt_ref[bb, ss * npg + j]
            pltpu.make_async_copy(ckv_hbm.at[pg], cbuf.at[sl, j], sems.at[0, sl]).start()
            pltpu.make_async_copy(krt_hbm.at[pg], kbuf.at[sl, j], sems.at[1, sl]).start()
            return carry

        lax.fori_loop(0, npg, body, 0, unroll=8)

    @pl.when(g_step == 0)
    def _():
        for g0 in range(n_slots - 1):
            @pl.when(g0 < total)
            def _():
                fetch(jnp.int32(g0))

    @pl.when(g_step + n_slots - 1 < total)
    def _():
        fetch(g_step + n_slots - 1)

    pltpu.make_async_copy(ckv_hbm.at[pl.ds(0, npg)], cbuf.at[slot], sems.at[0, slot]).wait()
    pltpu.make_async_copy(krt_hbm.at[pl.ds(0, npg)], kbuf.at[slot], sems.at[1, slot]).wait()

    @pl.when(step == 0)
    def _():
        m_ref[...] = jnp.full(m_ref.shape, NEG_BIG, F32)
        l_ref[...] = jnp.zeros(l_ref.shape, F32)
        acc_ref[...] = jnp.zeros(acc_ref.shape, F32)

    ql = qlat_ref[0]
    qr = qr_ref[0][:, :rope_dim]

    def update(s, vals):
        m_prev = m_ref[...]
        m_new = jnp.maximum(m_prev, jnp.max(s, axis=-1, keepdims=True))
        alpha = jnp.exp(m_prev - m_new)
        p = jnp.exp(s - m_new)
        l_ref[...] = alpha * l_ref[...] + jnp.sum(p, axis=-1, keepdims=True)
        acc_ref[...] = alpha * acc_ref[...] + _dot(p.astype(BF16), vals)
        m_ref[...] = m_new

    for j in range(npg):
        cb_ref[j * page:(j + 1) * page, :] = cbuf[slot, j].astype(BF16)
        kb_ref[:, j * page:(j + 1) * page] = kbuf[slot, j].astype(BF16)
    cb = cb_ref[...]
    update((_dot_nt(ql, cb) + _dot(qr, kb_ref[...])) * scale, cb)

    @pl.when(step == pl.num_programs(1) - 1)
    def _():
        cn = cnew_ref[0]
        krn = krnew_ref[0]
        s = (_dot_nt(ql, cn) + _dot_nt(qr, krn)) * scale
        t_row = lax.broadcasted_iota(jnp.int32, s.shape, 0) >> (n_heads.bit_length() - 1)
        col = lax.broadcasted_iota(jnp.int32, s.shape, 1)
        update(jnp.where(col <= t_row, s, NEG_BIG), cn)
        o_ref[0] = (acc_ref[...] / l_ref[...]).astype(BF16)


def _mla_decode(page_table, qlat, qr, cnew, krnew, cache_ckv, cache_krope_t, *, npg, scale, n_heads, rope_dim):
    nb, rows, kv_rank = qlat.shape
    n_pages = page_table.shape[1]
    page = cache_ckv.shape[1]
    steps = n_pages // npg
    per_b = lambda b, p, pt: (b, 0, 0)
    grid_spec = pltpu.PrefetchScalarGridSpec(
        num_scalar_prefetch=1,
        grid=(nb, steps),
        in_specs=[
            pl.BlockSpec((1, rows, kv_rank), per_b),
            pl.BlockSpec((1, rows, LANES), per_b),
            pl.BlockSpec((1,) + cnew.shape[1:], per_b),
            pl.BlockSpec((1,) + krnew.shape[1:], per_b),
            pl.BlockSpec(memory_space=pl.ANY),
            pl.BlockSpec(memory_space=pl.ANY),
        ],
        out_specs=pl.BlockSpec((1, rows, kv_rank), per_b),
        scratch_shapes=[
            pltpu.VMEM((DECODE_SLOTS, npg, page, kv_rank), F32),
            pltpu.VMEM((DECODE_SLOTS, npg, rope_dim, page), F32),
            pltpu.VMEM((npg * page, kv_rank), BF16),
            pltpu.VMEM((rope_dim, npg * page), BF16),
            pltpu.VMEM((rows, 1), F32),
            pltpu.VMEM((rows, 1), F32),
            pltpu.VMEM((rows, kv_rank), F32),
            pltpu.SemaphoreType.DMA((2, DECODE_SLOTS)),
        ],
    )
    return pl.pallas_call(
        functools.partial(_decode_kernel, npg=npg, scale=scale, n_heads=n_heads, rope_dim=rope_dim),
        grid_spec=grid_spec,
        out_shape=jax.ShapeDtypeStruct((nb, rows, kv_rank), BF16),
        compiler_params=_params("arbitrary", "arbitrary"),
    )(page_table, qlat, qr, cnew, krnew, cache_ckv, cache_krope_t)


def _head_proj_kernel(x_ref, w_ref, o_ref, *, n_heads):
    k = x_ref.shape[1] // n_heads
    n = o_ref.shape[1] // n_heads
    for h in range(n_heads):
        o_ref[:, h * n:(h + 1) * n] = _dot(x_ref[:, h * k:(h + 1) * k], w_ref[h]).astype(BF16)


def _head_proj(x, w, *, n_heads):
    rows = x.shape[0]
    n = w.shape[2]
    return pl.pallas_call(
        functools.partial(_head_proj_kernel, n_heads=n_heads),
        grid=(1,),
        in_specs=[pl.BlockSpec(x.shape, lambda i: (0, 0)), pl.BlockSpec(w.shape, lambda i: (0, 0, 0))],
        out_specs=pl.BlockSpec((rows, n_heads * n), lambda i: (0, 0)),
        out_shape=jax.ShapeDtypeStruct((rows, n_heads * n), BF16),
        compiler_params=_params("arbitrary"),
    )(x, w)


def _group_norm_gate(o, rg, gn_w):
    mu = jnp.mean(o, axis=-1, keepdims=True)
    d = o - mu
    var = jnp.mean(d * d, axis=-1, keepdims=True)
    on = d * lax.rsqrt(var + GN_EPS) * gn_w
    return (on * (rg * jax.nn.sigmoid(rg))).astype(BF16)


def _ret_prompt_kernel(q_ref, k_ref, v_ref, g_ref, cos_ref, sin_ref, lg_ref, gn_ref, o_ref, st_ref, state,
                       *, n_sub, chunk, k_scale):
    ci = pl.program_id(2)

    @pl.when(ci == 0)
    def _():
        state[...] = jnp.zeros(state.shape, F32)

    lg = lg_ref[0][:, :1]
    ii = lax.broadcasted_iota(jnp.int32, (chunk, chunk), 0)
    jj = lax.broadcasted_iota(jnp.int32, (chunk, chunk), 1)
    diff = (ii - jj).astype(F32)
    decay = jnp.where(diff >= 0, jnp.exp(jnp.maximum(diff, 0.0) * lg), 0.0)
    ri = lax.broadcasted_iota(jnp.int32, (chunk, 1), 0).astype(F32)
    xi = jnp.exp((ri + 1.0) * lg)
    to_end = jnp.exp((chunk - 1.0 - ri) * lg)
    g_chunk = jnp.exp(chunk * lg)
    gn_w = gn_ref[...]
    for sub in range(n_sub):
        sl = pl.ds(sub * chunk, chunk)
        cos = cos_ref[sl, :]
        sin = sin_ref[sl, :]
        q = _rope128(q_ref[sl, :], cos, sin)
        k = _rope128(k_ref[sl, :], cos, sin) * k_scale
        qb = q.astype(BF16)
        vb = v_ref[sl, :].astype(BF16)
        s = _dot_nt(qb, k.astype(BF16)) * decay
        st = state[...]
        o = _dot(s.astype(BF16), vb) + _dot(qb, st.astype(BF16)) * xi
        kw_t = jnp.transpose(k * to_end).astype(BF16)
        state[...] = g_chunk * st + _dot(kw_t, vb)
        o_ref[sl, :] = _group_norm_gate(o, g_ref[sl, :], gn_w)

    @pl.when(ci == pl.num_programs(2) - 1)
    def _():
        st_ref[0, 0] = state[...]


def _ret_prompt(z, cos128, sin128, lgam, gn_w, *, batch, seq, n_heads, dk, dv, rows, offs, total_rows):
    rq_off, rk_off, rv_off, rg_off = offs
    nr = seq // rows
    qk_spec = lambda off: pl.BlockSpec((rows, dk), lambda b, h, c: (b * nr + c, off // dk + h))
    v_spec = lambda off: pl.BlockSpec((rows, dv), lambda b, h, c: (b * nr + c, off // dv + h))
    tab = pl.BlockSpec((rows, LANES), lambda b, h, c: (b * nr + c, 0))
    return pl.pallas_call(
        functools.partial(_ret_prompt_kernel, n_sub=rows // RET_CHUNK, chunk=RET_CHUNK, k_scale=dk ** -0.5),
        grid=(batch, n_heads, nr),
        in_specs=[
            qk_spec(rq_off), qk_spec(rk_off), v_spec(rv_off), v_spec(rg_off), tab, tab,
            pl.BlockSpec((1, 1, LANES), lambda b, h, c: (h, 0, 0)),
            pl.BlockSpec((1, dv), lambda b, h, c: (0, h)),
        ],
        out_specs=[
            pl.BlockSpec((rows, dv), lambda b, h, c: (b * nr + c, h)),
            pl.BlockSpec((1, 1, dk, dv), lambda b, h, c: (b, h, 0, 0)),
        ],
        out_shape=[
            jax.ShapeDtypeStruct((total_rows, n_heads * dv), BF16),
            jax.ShapeDtypeStruct((batch, n_heads, dk, dv), F32),
        ],
        scratch_shapes=[pltpu.VMEM((dk, dv), F32)],
        compiler_params=_params("parallel", "parallel", "arbitrary"),
    )(z, z, z, z, cos128, sin128, lgam, gn_w)


def _ret_sample_kernel(q_ref, k_ref, v_ref, g_ref, cos_ref, sin_ref, lg_ref, gn_ref, st_in_ref, o_ref, st_out_ref,
                       *, nb, t_log2, k_scale):
    ts = 1 << t_log2
    rows = nb * ts
    lg = lg_ref[0][:, :1]
    ii = lax.broadcasted_iota(jnp.int32, (rows, rows), 0)
    jj = lax.broadcasted_iota(jnp.int32, (rows, rows), 1)
    diff = (ii - jj).astype(F32)
    keep = ((ii >> t_log2) == (jj >> t_log2)) & (ii >= jj)
    decay = jnp.where(keep, jnp.exp(jnp.maximum(diff, 0.0) * lg), 0.0)
    ri = lax.broadcasted_iota(jnp.int32, (rows, 1), 0)
    rb = ri >> t_log2
    ti = (ri & (ts - 1)).astype(F32)
    xi = jnp.exp((ti + 1.0) * lg)
    to_end = jnp.exp((ts - 1.0 - ti) * lg)
    g_chunk = jnp.exp(ts * lg)
    cos = cos_ref[...]
    sin = sin_ref[...]
    q = _rope128(q_ref[...], cos, sin)
    k = _rope128(k_ref[...], cos, sin) * k_scale
    qb = q.astype(BF16)
    vb = v_ref[...].astype(BF16)
    s = _dot_nt(qb, k.astype(BF16)) * decay
    o_in = _dot(s.astype(BF16), vb)
    kw = k * to_end
    o_cross = jnp.zeros(o_in.shape, F32)
    for b in range(nb):
        st = st_in_ref[b, 0]
        mine = rb == b
        o_cross = jnp.where(mine, _dot(qb, st.astype(BF16)), o_cross)
        kw_t = jnp.transpose(jnp.where(mine, kw, 0.0)).astype(BF16)
        st_out_ref[b, 0] = g_chunk * st + _dot(kw_t, vb)
    o_ref[...] = _group_norm_gate(o_in + o_cross * xi, g_ref[...], gn_ref[...])


def _ret_sample(z, cos128, sin128, lgam, gn_w, state, *, row0, dec_batch, dec_seq, n_heads, dk, dv, offs):
    rq_off, rk_off, rv_off, rg_off = offs
    t_log2 = dec_seq.bit_length() - 1
    assert 1 << t_log2 == dec_seq
    rows = LANES
    nb = rows // dec_seq
    nblk = dec_batch // nb
    rb0 = row0 // rows
    qk_spec = lambda off: pl.BlockSpec((rows, dk), lambda i, h: (rb0 + i, off // dk + h))
    v_spec = lambda off: pl.BlockSpec((rows, dv), lambda i, h: (rb0 + i, off // dv + h))
    tab = pl.BlockSpec((rows, LANES), lambda i, h: (rb0 + i, 0))
    return pl.pallas_call(
        functools.partial(_ret_sample_kernel, nb=nb, t_log2=t_log2, k_scale=dk ** -0.5),
        grid=(nblk, n_heads),
        in_specs=[
            qk_spec(rq_off), qk_spec(rk_off), v_spec(rv_off), v_spec(rg_off), tab, tab,
            pl.BlockSpec((1, 1, LANES), lambda i, h: (h, 0, 0)),
            pl.BlockSpec((1, dv), lambda i, h: (0, h)),
            pl.BlockSpec((nb, 1, dk, dv), lambda i, h: (i, h, 0, 0)),
        ],
        out_specs=[
            pl.BlockSpec((rows, dv), lambda i, h: (i, h)),
            pl.BlockSpec((nb, 1, dk, dv), lambda i, h: (i, h, 0, 0)),
        ],
        out_shape=[
            jax.ShapeDtypeStruct((dec_batch * dec_seq, n_heads * dv), BF16),
            jax.ShapeDtypeStruct(state.shape, F32),
        ],
        compiler_params=_params("parallel", "parallel"),
    )(z, z, z, z, cos128, sin128, lgam, gn_w, state)


def _softmax_rows(s):
    e = jnp.exp(s - jnp.max(s, axis=-1, keepdims=True))
    return e / jnp.sum(e, axis=-1, keepdims=True)


def _mem_attn_prompt_kernel(q_ref, k_ref, v_ref, o_ref, *, scale):
    s = _dot_nt(q_ref[...].astype(BF16), k_ref[...].astype(BF16)) * scale
    o_ref[...] = _dot(_softmax_rows(s).astype(BF16), v_ref[...].astype(BF16)).astype(BF16)


def _mem_attn_prompt(z, kv, *, batch, seq, mem_tokens, n_heads, hd, tq, mq_off, total_rows):
    nq = seq // tq
    return pl.pallas_call(
        functools.partial(_mem_attn_prompt_kernel, scale=hd ** -0.5),
        grid=(batch, n_heads, nq),
        in_specs=[
            pl.BlockSpec((tq, hd), lambda b, h, i: (b * nq + i, mq_off // hd + h)),
            pl.BlockSpec((mem_tokens, hd), lambda b, h, i: (b, h)),
            pl.BlockSpec((mem_tokens, hd), lambda b, h, i: (b, n_heads + h)),
        ],
        out_specs=pl.BlockSpec((tq, hd), lambda b, h, i: (b * nq + i, h)),
        out_shape=jax.ShapeDtypeStruct((total_rows, n_heads * hd), BF16),
        compiler_params=_params("parallel", "parallel", "parallel"),
    )(z, kv, kv)


def _mem_attn_sample_kernel(q_ref, k_ref, v_ref, o_ref, *, nb, t_log2, scale):
    q = q_ref[...].astype(BF16)
    rb = lax.broadcasted_iota(jnp.int32, (q.shape[0], 1), 0) >> t_log2
    o = jnp.zeros(o_ref.shape, F32)
    for b in range(nb):
        s = _dot_nt(q, k_ref[b].astype(BF16)) * scale
        ob = _dot(_softmax_rows(s).astype(BF16), v_ref[b].astype(BF16))
        o = jnp.where(rb == b, ob, o)
    o_ref[...] = o.astype(BF16)


def _mem_attn_sample(z, mem_k, mem_v, *, row0, dec_batch, dec_seq, n_heads, hd, nb, mq_off):
    t_log2 = dec_seq.bit_length() - 1
    rows = nb * dec_seq
    rb0 = row0 // rows
    mem_tokens = mem_k.shape[1]
    return pl.pallas_call(
        functools.partial(_mem_attn_sample_kernel, nb=nb, t_log2=t_log2, scale=hd ** -0.5),
        grid=(dec_batch // nb, n_heads),
        in_specs=[
            pl.BlockSpec((rows, hd), lambda i, h: (rb0 + i, mq_off // hd + h)),
            pl.BlockSpec((nb, mem_tokens, hd), lambda i, h: (i, 0, h)),
            pl.BlockSpec((nb, mem_tokens, hd), lambda i, h: (i, 0, h)),
        ],
        out_specs=pl.BlockSpec((rows, hd), lambda i, h: (i, h)),
        out_shape=jax.ShapeDtypeStruct((dec_batch * dec_seq, n_heads * hd), BF16),
        compiler_params=_params("parallel", "parallel"),
    )(z, mem_k, mem_v)


def _merge_kernel(oap_ref, obp_ref, ocp_ref, oas_ref, obs_ref, ocs_ref, wa_ref, wb_ref, wc_ref, ga_ref, gb_ref,
                  gc_ref, m_ref, *, n_first):
    def run(oa_ref, ob_ref, oc_ref):
        merged = jax.nn.sigmoid(ga_ref[...]) * _dot(oa_ref[...], wa_ref[...])
        merged = merged + jax.nn.sigmoid(gb_ref[...]) * _dot(ob_ref[...], wb_ref[...])
        merged = merged + jax.nn.sigmoid(gc_ref[...]) * _dot(oc_ref[...], wc_ref[...])
        m_ref[...] = merged.astype(BF16)

    i = pl.program_id(1)

    @pl.when(i < n_first)
    def _():
        run(oap_ref, obp_ref, ocp_ref)

    @pl.when(i >= n_first)
    def _():
        run(oas_ref, obs_ref, ocs_ref)


def _merge(prompt_outs, sample_outs, w_a, w_b, w_c, z, *, tm, tn, gz_off):
    tp = prompt_outs[0].shape[0]
    t = tp + sample_outs[0].shape[0]
    n_first = tp // tm
    d = w_a.shape[1]
    nj = d // tn
    act_p = lambda a: pl.BlockSpec((tm, a.shape[1]), lambda j, i: (jnp.minimum(i, n_first - 1), 0))
    act_s = lambda a: pl.BlockSpec((tm, a.shape[1]), lambda j, i: (jnp.maximum(i - n_first, 0), 0))
    wsp = lambda w: pl.BlockSpec((w.shape[0], tn), lambda j, i: (0, j))
    gate = lambda br: pl.BlockSpec((tm, tn), lambda j, i: (i, (gz_off + br * d) // tn + j))
    return pl.pallas_call(
        functools.partial(_merge_kernel, n_first=n_first),
        grid=(nj, t // tm),
        in_specs=[act_p(a) for a in prompt_outs] + [act_s(a) for a in sample_outs]
        + [wsp(w_a), wsp(w_b), wsp(w_c), gate(0), gate(1), gate(2)],
        out_specs=pl.BlockSpec((tm, tn), lambda j, i: (i, j)),
        out_shape=jax.ShapeDtypeStruct((t, d), BF16),
        compiler_params=_params("parallel", "parallel"),
    )(*prompt_outs, *sample_outs, w_a, w_b, w_c, z, z, z)


def _out_proj_kernel(x_ref, m_ref, wo_ref, nw_ref, wrh_ref, wrl_ref, br_ref, x1_ref, h_ref, lg_ref):
    x1 = x_ref[...] + _dot(m_ref[...], wo_ref[...])
    x1_ref[...] = x1
    h = _rms(x1, nw_ref[...])
    h_ref[...] = h
    h_hi = h.astype(BF16)
    h_lo = (h - h_hi.astype(F32)).astype(BF16)
    w_hi = wrh_ref[...]
    lg_ref[...] = _dot(h_hi, w_hi) + _dot(h_lo, w_hi) + _dot(h_hi, wrl_ref[...]) + br_ref[...]


def _out_proj(x, merged, w_o, norm_w, w_router_hi, w_router_lo, b_router, *, tm):
    t, d = x.shape
    row = lambda i: (i, 0)
    const = lambda i: (0, 0)
    return pl.pallas_call(
        _out_proj_kernel,
        grid=(t // tm,),
        in_specs=[
            pl.BlockSpec((tm, d), row),
            pl.BlockSpec((tm, d), row),
            pl.BlockSpec((d, d), const),
            pl.BlockSpec((1, d), const),
            pl.BlockSpec((d, LANES), const),
            pl.BlockSpec((d, LANES), const),
            pl.BlockSpec((1, LANES), const),
        ],
        out_specs=[pl.BlockSpec((tm, d), row), pl.BlockSpec((tm, d), row), pl.BlockSpec((tm, LANES), row)],
        out_shape=[
            jax.ShapeDtypeStruct((t, d), F32),
            jax.ShapeDtypeStruct((t, d), F32),
            jax.ShapeDtypeStruct((t, LANES), F32),
        ],
        compiler_params=_params("parallel"),
    )(x, merged, w_o, norm_w, w_router_hi, w_router_lo, b_router)


def _gather_rows(src_hbm, idx_ref, base, buf, sem, n_rows):
    def issue(r, carry):
        pltpu.make_async_copy(src_hbm.at[pl.ds(idx_ref[base + r], 1), :], buf.at[pl.ds(r, 1), :], sem).start()
        return carry

    lax.fori_loop(0, n_rows, issue, 0, unroll=8)


def _wait_rows(src_hbm, buf, sem, n_rows):
    pltpu.make_async_copy(src_hbm.at[pl.ds(0, n_rows), :], buf, sem).wait()


def _start_rows(src_hbm, idx_ref, base, buf, sem, n_rows):
    for r in range(n_rows):
        pltpu.make_async_copy(src_hbm.at[pl.ds(idx_ref[base + r], 1), :], buf.at[pl.ds(r, 1), :], sem).start()


def _experts_kernel(be_ref, nu_ref, tok_ref, h_hbm, wg_ref, wu_ref, wd_ref, y_ref, buf_a, buf_b, wg_b, wu_b, wd_b,
                    sems, *, blk):
    i = pl.program_id(0)
    n_used = nu_ref[0]

    @pl.when(i == 0)
    def _():
        _gather_rows(h_hbm, tok_ref, 0, buf_a, sems.at[0], blk)

    @pl.when((i < n_used) & ((i == 0) | (be_ref[i] != be_ref[jnp.maximum(i - 1, 0)])))
    def _():
        wg_b[...] = wg_ref[...].astype(BF16)
        wu_b[...] = wu_ref[...].astype(BF16)
        wd_b[...] = wd_ref[...].astype(BF16)

    def run(cur, nxt, sem_cur, sem_nxt):
        _wait_rows(h_hbm, cur, sem_cur, blk)
        _start_rows(h_hbm, tok_ref, jnp.minimum(i + 1, n_used - 1) * blk, nxt, sem_nxt, blk)
        xb = cur[...].astype(BF16)
        g = _dot(xb, wg_b[...])
        u = _dot(xb, wu_b[...])
        a = (g * jax.nn.sigmoid(g) * u).astype(BF16)
        y_ref[...] = _dot(a, wd_b[...])

        @pl.when(i == n_used - 1)
        def _():
            _wait_rows(h_hbm, nxt, sem_nxt, blk)

    @pl.when((i < n_used) & (i % 2 == 0))
    def _():
        run(buf_a, buf_b, sems.at[0], sems.at[1])

    @pl.when((i < n_used) & (i % 2 == 1))
    def _():
        run(buf_b, buf_a, sems.at[1], sems.at[0])

    @pl.when(i >= n_used)
    def _():
        y_ref[...] = jnp.zeros(y_ref.shape, F32)


def _experts(blk_expert, n_used, row_tok, h, w_gate, w_up, w_down, *, n_blk, blk):
    d = h.shape[1]
    de = w_gate.shape[2]
    grid_spec = pltpu.PrefetchScalarGridSpec(
        num_scalar_prefetch=3,
        grid=(n_blk,),
        in_specs=[
            pl.BlockSpec(memory_space=pl.ANY),
            pl.BlockSpec((None, d, de), lambda i, be, nu, tok: (be[i], 0, 0)),
            pl.BlockSpec((None, d, de), lambda i, be, nu, tok: (be[i], 0, 0)),
            pl.BlockSpec((None, de, d), lambda i, be, nu, tok: (be[i], 0, 0)),
        ],
        out_specs=pl.BlockSpec((blk, d), lambda i, be, nu, tok: (i, 0)),
        scratch_shapes=[
            pltpu.VMEM((blk, d), F32),
            pltpu.VMEM((blk, d), F32),
            pltpu.VMEM((d, de), BF16),
            pltpu.VMEM((d, de), BF16),
            pltpu.VMEM((de, d), BF16),
            pltpu.SemaphoreType.DMA((2,)),
        ],
    )
    return pl.pallas_call(
        functools.partial(_experts_kernel, blk=blk),
        grid_spec=grid_spec,
        out_shape=jax.ShapeDtypeStruct((n_blk * blk, d), F32),
        compiler_params=_params("arbitrary"),
    )(blk_expert, n_used, row_tok, h, w_gate, w_up, w_down)


def _combine_kernel(d0_ref, d1_ref, y_hbm, x1_ref, g0_ref, g1_ref, nw_ref, op_ref, os_ref, b0a, b0b, b1a, b1b, sems,
                    *, tm, n_first):
    i = pl.program_id(0)
    last = pl.num_programs(0) - 1

    @pl.when(i == 0)
    def _():
        _gather_rows(y_hbm, d0_ref, 0, b0a, sems.at[0, 0], tm)
        _gather_rows(y_hbm, d1_ref, 0, b1a, sems.at[1, 0], tm)

    def run(cur0, cur1, nxt0, nxt1, s):
        _wait_rows(y_hbm, cur0, sems.at[0, s], tm)
        _wait_rows(y_hbm, cur1, sems.at[1, s], tm)
        base = jnp.minimum(i + 1, last) * tm
        _start_rows(y_hbm, d0_ref, base, nxt0, sems.at[0, 1 - s], tm)
        _start_rows(y_hbm, d1_ref, base, nxt1, sems.at[1, 1 - s], tm)
        reps = x1_ref.shape[1] // LANES
        g0 = jnp.tile(g0_ref[...], (1, reps))
        g1 = jnp.tile(g1_ref[...], (1, reps))
        x2 = x1_ref[...] + (cur0[...] * g0 + cur1[...] * g1)
        y = _rms(x2, nw_ref[...])

        @pl.when(i < n_first)
        def _():
            op_ref[...] = y

        @pl.when(i >= n_first)
        def _():
            os_ref[...] = y

        @pl.when(i == last)
        def _():
            _wait_rows(y_hbm, nxt0, sems.at[0, 1 - s], tm)
            _wait_rows(y_hbm, nxt1, sems.at[1, 1 - s], tm)

    @pl.when(i % 2 == 0)
    def _():
        run(b0a, b1a, b0b, b1b, 0)

    @pl.when(i % 2 == 1)
    def _():
        run(b0b, b1b, b0a, b1a, 1)


def _combine(d0, d1, y_rows, x1, g0, g1, norm_w, *, tm, rows_first):
    t, d = x1.shape
    n_first = rows_first // tm
    row = lambda i, a, b: (i, 0)
    grid_spec = pltpu.PrefetchScalarGridSpec(
        num_scalar_prefetch=2,
        grid=(t // tm,),
        in_specs=[
            pl.BlockSpec(memory_space=pl.ANY),
            pl.BlockSpec((tm, d), row),
            pl.BlockSpec((tm, LANES), row),
            pl.BlockSpec((tm, LANES), row),
            pl.BlockSpec((1, d), lambda i, a, b: (0, 0)),
        ],
        out_specs=[
            pl.BlockSpec((tm, d), lambda i, a, b: (jnp.minimum(i, n_first - 1), 0)),
            pl.BlockSpec((tm, d), lambda i, a, b: (jnp.maximum(i - n_first, 0), 0)),
        ],
        scratch_shapes=[pltpu.VMEM((tm, d), F32)] * 4 + [pltpu.SemaphoreType.DMA((2, 2))],
    )
    return pl.pallas_call(
        functools.partial(_combine_kernel, tm=tm, n_first=n_first),
        grid_spec=grid_spec,
        out_shape=[jax.ShapeDtypeStruct((rows_first, d), F32), jax.ShapeDtypeStruct((t - rows_first, d), F32)],
        compiler_params=_params("arbitrary"),
    )(d0, d1, y_rows, x1, g0, g1, norm_w)


def _route(logits, n_groups, per_group, top_k, blk):
    t = logits.shape[0]
    n_exp = n_groups * per_group
    assert top_k == 2
    grp_p = jax.nn.softmax(logits[:, :n_groups], axis=-1)
    g_idx = jnp.argmax(grp_p, axis=-1, keepdims=True)
    g_prob = jnp.max(grp_p, axis=-1, keepdims=True)
    e_logit = logits[:, n_groups:n_groups + n_exp].reshape(t, n_groups, per_group)
    in_grp = jnp.take_along_axis(e_logit, g_idx[:, :, None], axis=1)[:, 0, :]
    i1 = jnp.argmax(in_grp, axis=-1, keepdims=True)
    rest = jnp.where(jnp.arange(per_group)[None, :] == i1, -jnp.inf, in_grp)
    i2 = jnp.argmax(rest, axis=-1, keepdims=True)
    top_i = jnp.concatenate([i1, i2], axis=-1)
    top_v = jnp.concatenate([jnp.max(in_grp, axis=-1, keepdims=True), jnp.max(rest, axis=-1, keepdims=True)], axis=-1)
    gate = g_prob * jax.nn.softmax(top_v, axis=-1)
    expert = (g_idx * per_group + top_i).reshape(-1).astype(jnp.int32)
    a = t * top_k
    onehot = expert[:, None] == jnp.arange(n_exp, dtype=jnp.int32)[None, :]
    csum = jnp.cumsum(onehot.astype(jnp.int32), axis=0)
    counts = csum[-1]
    rank = jnp.sum(jnp.where(onehot, csum - 1, 0), axis=1)
    padded = (counts + blk - 1) // blk * blk
    pad_end = jnp.cumsum(padded)
    pad_start = pad_end - padded
    dest = (pad_start[expert] + rank).astype(jnp.int32)
    n_blk = (a + n_exp * (blk - 1) + blk - 1) // blk
    tok = jnp.arange(a, dtype=jnp.int32) // top_k
    row_tok = jnp.zeros((n_blk * blk,), jnp.int32).at[dest].set(tok)
    blk_start = jnp.arange(n_blk, dtype=jnp.int32) * blk
    blk_expert = jnp.minimum(jnp.sum(pad_end[None, :] <= blk_start[:, None], axis=1), n_exp - 1).astype(jnp.int32)
    n_used = (pad_end[-1:] // blk).astype(jnp.int32)
    return gate, dest.reshape(t, top_k), row_tok, blk_expert, n_used, n_blk


def _pick(n, *prefs):
    for p in prefs:
        if n % p == 0:
            return p
    return n


def kernel(x_prompt, x_sample, mem_prompt, cache_ckv, cache_krope, state_ret, cache_mem_k, cache_mem_v, page_table, attn_norm_w, w_in, mla_q_norm_w, mla_w_uq, mla_kv_norm_w, mla_w_uk, mla_w_uv, ret_gn_w, mem_norm_w, mem_w_kv, w_branch_a, w_branch_b, w_branch_c, w_out, ffn_norm_w, router_grp_w, router_grp_b, router_exp_w, router_exp_b, exp_w_gate, exp_w_up, exp_w_down, final_norm_w):
    batch, seq, d = x_prompt.shape
    dec_batch, dec_seq, _ = x_sample.shape
    depth = w_in.shape[0]
    q_rank, n_heads, qh = mla_w_uq.shape[1:]
    kv_rank, _, nope = mla_w_uk.shape[1:]
    rope_dim = qh - nope
    v_dim = mla_w_uv.shape[3]
    ret_heads, ret_dv = ret_gn_w.shape[1:]
    ret_dk = state_ret.shape[3]
    mem_tokens = mem_prompt.shape[1]
    mem_heads, mem_hd = cache_mem_k.shape[3:]
    mem_width = mem_heads * mem_hd
    n_groups = router_grp_w.shape[2]
    n_exp = router_exp_w.shape[2]
    per_group = n_exp // n_groups
    top_k = 2
    page = cache_ckv.shape[2]
    past_len = page_table.shape[1] * page
    assert nope == LANES and v_dim == LANES and rope_dim == 64 and ret_dk == LANES
    assert q_rank == kv_rank and n_groups + n_exp <= LANES

    tp = batch * seq
    ts = dec_batch * dec_seq
    t = tp + ts
    ret_qk = ret_heads * ret_dk
    ret_v = ret_heads * ret_dv
    cq_off, ckv_off = 0, q_rank
    rq_off = ckv_off + kv_rank
    rk_off = rq_off + ret_qk
    rv_off = rk_off + ret_qk
    rg_off = rv_off + ret_v
    mq_off = rg_off + ret_v
    gz_off = mq_off + mem_width
    ret_offs = (rq_off, rk_off, rv_off, rg_off)

    pos = jnp.concatenate([jnp.tile(jnp.arange(seq), batch), jnp.tile(past_len + jnp.arange(dec_seq), dec_batch)])
    cos64, sin64 = _rope_tables(pos, rope_dim)
    cos128, sin128 = _rope_tables(pos, ret_dk)
    lgam = jnp.log1p(-jnp.exp2(-5.0 - jnp.arange(ret_heads, dtype=F32)))
    lgam = jnp.broadcast_to(lgam[:, None, None], (ret_heads, 1, LANES))

    xp = x_prompt.reshape(tp, d)
    xs = x_sample.reshape(ts, d)
    x = jnp.concatenate([xp, xs], axis=0)

    assert depth == 1, "the fused final norm assumes a single layer"
    l = 0
    tm_mid = _pick(t, 256)

    wi = w_in[l]
    kr_lo = q_rank + kv_rank
    w_main = jnp.concatenate([wi[:, :kr_lo], wi[:, kr_lo + rope_dim:]], axis=1).astype(BF16)
    w_kr = jnp.pad(wi[:, kr_lo:kr_lo + rope_dim], ((0, 0), (0, LANES - rope_dim))).astype(BF16)
    z, zk = _norm_proj(x, attn_norm_w[l], w_main, w_kr, tm=_pick(t, 1536, 1024, 512),
                       tn=_pick(w_main.shape[1], 512))

    wuq = mla_w_uq[l]
    wqn = wuq[:, :, :nope].reshape(q_rank, n_heads * LANES).astype(BF16)
    wqr = jnp.pad(wuq[:, :, nope:], ((0, 0), (0, 0), (0, LANES - rope_dim))).reshape(q_rank, n_heads * LANES).astype(BF16)
    wuk = mla_w_uk[l].reshape(kv_rank, n_heads * nope).astype(BF16)
    wuv = mla_w_uv[l].reshape(kv_rank, n_heads * v_dim).astype(BF16)
    qcat, c, kr, kcat, v = _mla_prep(
        z, zk, cos64, sin64, mla_q_norm_w[l].reshape(1, -1), mla_kv_norm_w[l].reshape(1, -1), wqn, wqr, wuk, wuv,
        tm=tm_mid, n_heads=n_heads, q_rank=q_rank, kv_rank=kv_rank, rope_dim=rope_dim)

    scale = qh ** -0.5
    oa_p = _mla_prompt(qcat, kcat, v, batch=batch, seq=seq, n_heads=n_heads, tq=_pick(seq, 512), scale=scale,
                       total_rows=tp)
    wukt = jnp.transpose(mla_w_uk[l], (1, 2, 0)).astype(BF16)
    qlat, qr = _absorb_q(qcat, wukt, ts=ts, row_block=tp // ts, n_heads=n_heads, kv_rank=kv_rank)
    rows = dec_seq * n_heads
    new_pad = LANES - dec_seq
    cnew = jnp.pad(c[tp:].reshape(dec_batch, dec_seq, kv_rank), ((0, 0), (0, new_pad), (0, 0))).astype(BF16)
    krnew = jnp.pad(kr[tp:].reshape(dec_batch, dec_seq, rope_dim), ((0, 0), (0, new_pad), (0, 0))).astype(BF16)
    o_lat = _mla_decode(
        page_table, qlat.reshape(dec_batch, rows, kv_rank), qr.reshape(dec_batch, rows, LANES), cnew, krnew,
        cache_ckv[l], jnp.swapaxes(cache_krope[l], 1, 2), npg=_pick(page_table.shape[1], 32), scale=scale,
        n_heads=n_heads, rope_dim=rope_dim)
    wuv_h = jnp.transpose(mla_w_uv[l], (1, 0, 2)).astype(BF16)
    oa_s = _head_proj(o_lat.reshape(ts, n_heads * kv_rank), wuv_h, n_heads=n_heads)

    gn_w = ret_gn_w[l].reshape(1, ret_v)
    ob_p, st_p = _ret_prompt(z, cos128, sin128, lgam, gn_w, batch=batch, seq=seq, n_heads=ret_heads, dk=ret_dk,
                             dv=ret_dv, rows=_pick(seq, 1024, 512), offs=ret_offs, total_rows=tp)
    ob_s, st_s = _ret_sample(z, cos128, sin128, lgam, gn_w, state_ret[l], row0=tp, dec_batch=dec_batch,
                             dec_seq=dec_seq, n_heads=ret_heads, dk=ret_dk, dv=ret_dv, offs=ret_offs)

    mem_rows = batch * mem_tokens
    (kv,) = _norm_proj(mem_prompt.reshape(mem_rows, d), mem_norm_w[l], mem_w_kv[l].astype(BF16),
                       tm=_pick(mem_rows, 1024, 512), tn=_pick(2 * mem_width, 512))
    oc_p = _mem_attn_prompt(z, kv, batch=batch, seq=seq, mem_tokens=mem_tokens, n_heads=mem_heads, hd=mem_hd,
                            tq=_pick(seq, 512), mq_off=mq_off, total_rows=tp)
    oc_s = _mem_attn_sample(z, cache_mem_k[l].reshape(dec_batch, mem_tokens, mem_width).astype(BF16),
                            cache_mem_v[l].reshape(dec_batch, mem_tokens, mem_width).astype(BF16), row0=tp,
                            dec_batch=dec_batch, dec_seq=dec_seq, n_heads=mem_heads, hd=mem_hd, nb=8, mq_off=mq_off)

    tm_merge = _pick(ts, 512, 256, 128)
    assert tp % tm_merge == 0
    merged = _merge((oa_p, ob_p, oc_p), (oa_s, ob_s, oc_s), w_branch_a[l].astype(BF16), w_branch_b[l].astype(BF16),
                    w_branch_c[l].astype(BF16), z, tm=tm_merge, tn=_pick(d, 1024), gz_off=gz_off)
    n_route = n_groups + n_exp
    w_router = jnp.pad(jnp.concatenate([router_grp_w[l], router_exp_w[l]], axis=1), ((0, 0), (0, LANES - n_route)))
    w_router_hi = w_router.astype(BF16)
    w_router_lo = (w_router - w_router_hi.astype(F32)).astype(BF16)
    b_router = jnp.pad(jnp.concatenate([router_grp_b[l], router_exp_b[l]]), (0, LANES - n_route)).reshape(1, LANES)
    x1, h2, logits = _out_proj(x, merged, w_out[l].astype(BF16), ffn_norm_w[l].reshape(1, d), w_router_hi,
                               w_router_lo, b_router, tm=tm_mid)

    gate, dest, row_tok, blk_expert, n_used, n_blk = _route(logits, n_groups, per_group, top_k, MOE_BLOCK)
    y_rows = _experts(blk_expert, n_used, row_tok, h2, exp_w_gate[l], exp_w_up[l], exp_w_down[l], n_blk=n_blk,
                      blk=MOE_BLOCK)
    g0 = jnp.broadcast_to(gate[:, 0:1], (t, LANES))
    g1 = jnp.broadcast_to(gate[:, 1:2], (t, LANES))
    tm_c = _pick(ts, 128)
    assert tp % tm_c == 0
    y_p, y_s = _combine(dest[:, 0], dest[:, 1], y_rows, x1, g0, g1, final_norm_w.reshape(1, d), tm=tm_c, rows_first=tp)

    lead = lambda a: a[None]
    return (
        y_p.reshape(batch, seq, d),
        y_s.reshape(dec_batch, dec_seq, d),
        lead(c[:tp].reshape(batch, seq, kv_rank)),
        lead(kr[:tp].reshape(batch, seq, rope_dim)),
        lead(st_p),
        lead(kv[:, :mem_width].reshape(batch, mem_tokens, mem_heads, mem_hd)),
        lead(kv[:, mem_width:].reshape(batch, mem_tokens, mem_heads, mem_hd)),
        lead(c[tp:].reshape(dec_batch, dec_seq, kv_rank)),
        lead(kr[tp:].reshape(dec_batch, dec_seq, rope_dim)),
        lead(st_s),
    )
```

```python
import functools

import jax
import jax.numpy as jnp
from jax import lax
from jax.experimental import pallas as pl
from jax.experimental.pallas import tpu as pltpu

F32 = jnp.float32
BF16 = jnp.bfloat16

NORM_EPS = 1e-6
GN_EPS = 1e-5
ROPE_THETA = 10000.0
RET_CHUNK = 128
MOE_BLOCK = 128
DECODE_SLOTS = 3
LANES = 128
NEG_BIG = -1e30
LOG2_E = 1.4426950408889634
VMEM_LIMIT_BYTES = 56 * 1024 * 1024


def _params(*sem):
    return pltpu.CompilerParams(dimension_semantics=sem, vmem_limit_bytes=VMEM_LIMIT_BYTES)


def _dot(a, b):
    return jnp.dot(a, b, preferred_element_type=F32)


def _dot_nt(a, b):
    return lax.dot_general(a, b, (((1,), (1,)), ((), ())), preferred_element_type=F32)


def _rms(x, w):
    return x * lax.rsqrt(jnp.mean(x * x, axis=-1, keepdims=True) + NORM_EPS) * w


def _rope64(x, cos, sin):
    lane = lax.broadcasted_iota(jnp.int32, x.shape, 1)
    first = (lane & 63) < 32
    swapped = jnp.where(first, pltpu.roll(x, 96, 1), pltpu.roll(x, 32, 1))
    return x * cos + swapped * sin


def _rope128(x, cos, sin):
    return x * cos + pltpu.roll(x, 64, 1) * sin


def _rope_tables(pos, dim):
    half = dim // 2
    inv = ROPE_THETA ** (-jnp.arange(half, dtype=F32) / half)
    ang = pos.astype(F32)[:, None] * inv[None, :]
    cos = jnp.cos(ang)
    sin = jnp.sin(ang)
    c = jnp.concatenate([cos, cos], axis=-1)
    s = jnp.concatenate([-sin, sin], axis=-1)
    reps = LANES // dim
    return jnp.tile(c, (1, reps)), jnp.tile(s, (1, reps))


def _norm_proj_kernel(x_ref, nw_ref, w_ref, z_ref, h_ref):
    @pl.when(pl.program_id(1) == 0)
    def _():
        h_ref[...] = _rms(x_ref[...], nw_ref[...]).astype(BF16)

    z_ref[...] = _dot(h_ref[...], w_ref[...])


def _norm_proj_extra_kernel(x_ref, nw_ref, w_ref, we_ref, z_ref, ze_ref, h_ref):
    @pl.when(pl.program_id(1) == 0)
    def _():
        h = _rms(x_ref[...], nw_ref[...]).astype(BF16)
        h_ref[...] = h
        ze_ref[...] = _dot(h, we_ref[...])

    z_ref[...] = _dot(h_ref[...], w_ref[...])


def _norm_proj(x, norm_w, w, w_extra=None, *, tm, tn):
    t, d = x.shape
    n = w.shape[1]
    grid = (t // tm, n // tn)
    in_specs = [
        pl.BlockSpec((tm, d), lambda i, j: (i, 0)),
        pl.BlockSpec((1, d), lambda i, j: (0, 0)),
        pl.BlockSpec((d, tn), lambda i, j: (0, j)),
    ]
    out_specs = [pl.BlockSpec((tm, tn), lambda i, j: (i, j))]
    out_shape = [jax.ShapeDtypeStruct((t, n), F32)]
    args = [x, norm_w.reshape(1, d), w]
    body = _norm_proj_kernel
    if w_extra is not None:
        ne = w_extra.shape[1]
        in_specs.append(pl.BlockSpec((d, ne), lambda i, j: (0, 0)))
        out_specs.append(pl.BlockSpec((tm, ne), lambda i, j: (i, 0)))
        out_shape.append(jax.ShapeDtypeStruct((t, ne), F32))
        args.append(w_extra)
        body = _norm_proj_extra_kernel
    return pl.pallas_call(
        body,
        grid=grid,
        in_specs=in_specs,
        out_specs=out_specs,
        out_shape=out_shape,
        scratch_shapes=[pltpu.VMEM((tm, d), BF16)],
        compiler_params=_params("parallel", "arbitrary"),
    )(*args)


def _mla_prep_kernel(cq_ref, ckv_ref, zk_ref, cos_ref, sin_ref, qnw_ref, kvnw_ref, wqn_ref, wqr_ref,
                     wuk_ref, wuv_ref, qcat_ref, c_ref, kr_ref, kcat_ref, v_ref, *, n_heads, rope_dim):
    cos = cos_ref[...]
    sin = sin_ref[...]
    cqn = _rms(cq_ref[...], qnw_ref[...]).astype(BF16)
    qn = _dot(cqn, wqn_ref[...])
    qr = _dot(cqn, wqr_ref[...])
    c = _rms(ckv_ref[...], kvnw_ref[...])
    c_ref[...] = c
    cb = c.astype(BF16)
    kn = _dot(cb, wuk_ref[...])
    v_ref[...] = _dot(cb, wuv_ref[...]).astype(BF16)
    kr = _rope64(zk_ref[...], cos, sin)
    kr_ref[...] = kr[:, :rope_dim]
    krb = kr.astype(BF16)
    for h in range(n_heads):
        lo = h * 2 * LANES
        sl = slice(h * LANES, (h + 1) * LANES)
        qcat_ref[:, lo:lo + LANES] = qn[:, sl].astype(BF16)
        qcat_ref[:, lo + LANES:lo + 2 * LANES] = _rope64(qr[:, sl], cos, sin).astype(BF16)
        kcat_ref[:, lo:lo + LANES] = kn[:, sl].astype(BF16)
        kcat_ref[:, lo + LANES:lo + 2 * LANES] = krb


def _mla_prep(z, zk, cos64, sin64, qnw, kvnw, wqn, wqr, wuk, wuv, *, tm, n_heads, q_rank, kv_rank, rope_dim):
    t = z.shape[0]
    row = lambda i: (i, 0)
    const = lambda i: (0, 0)
    hw = n_heads * LANES
    return pl.pallas_call(
        functools.partial(_mla_prep_kernel, n_heads=n_heads, rope_dim=rope_dim),
        grid=(t // tm,),
        in_specs=[
            pl.BlockSpec((tm, q_rank), lambda i: (i, 0)),
            pl.BlockSpec((tm, kv_rank), lambda i: (i, q_rank // kv_rank)),
            pl.BlockSpec((tm, LANES), row),
            pl.BlockSpec((tm, LANES), row),
            pl.BlockSpec((tm, LANES), row),
            pl.BlockSpec((1, q_rank), const),
            pl.BlockSpec((1, kv_rank), const),
            pl.BlockSpec((q_rank, hw), const),
            pl.BlockSpec((q_rank, hw), const),
            pl.BlockSpec((kv_rank, hw), const),
            pl.BlockSpec((kv_rank, hw), const),
        ],
        out_specs=[
            pl.BlockSpec((tm, 2 * hw), row),
            pl.BlockSpec((tm, kv_rank), row),
            pl.BlockSpec((tm, rope_dim), row),
            pl.BlockSpec((tm, 2 * hw), row),
            pl.BlockSpec((tm, hw), row),
        ],
        out_shape=[
            jax.ShapeDtypeStruct((t, 2 * hw), BF16),
            jax.ShapeDtypeStruct((t, kv_rank), F32),
            jax.ShapeDtypeStruct((t, rope_dim), F32),
            jax.ShapeDtypeStruct((t, 2 * hw), BF16),
            jax.ShapeDtypeStruct((t, hw), BF16),
        ],
        compiler_params=_params("parallel"),
    )(z, z, zk, cos64, sin64, qnw, kvnw, wqn, wqr, wuk, wuv)


def _flash_kernel(q_ref, k_ref, v_ref, o_ref, *, tq, scale):
    qi = pl.program_id(2)
    q = q_ref[...]

    def block(j, carry, diagonal):
        m, l, acc = carry
        start = pl.multiple_of(j * tq, tq)
        k = k_ref[pl.ds(start, tq), :]
        v = v_ref[pl.ds(start, tq), :]
        s = _dot_nt(q, k) * (scale * LOG2_E)
        if diagonal:
            rows = lax.broadcasted_iota(jnp.int32, (tq, tq), 0)
            cols = lax.broadcasted_iota(jnp.int32, (tq, tq), 1)
            s = jnp.where(cols <= rows, s, NEG_BIG)
        m_new = jnp.maximum(m, jnp.max(s, axis=-1, keepdims=True))
        alpha = jnp.exp2(m - m_new)
        p = jnp.exp2(s - m_new)
        l = alpha * l + jnp.sum(p, axis=-1, keepdims=True)
        acc = alpha * acc + _dot(p.astype(BF16), v)
        return m_new, l, acc

    dv = v_ref.shape[-1]
    init = (jnp.full((tq, 1), NEG_BIG, F32), jnp.zeros((tq, 1), F32), jnp.zeros((tq, dv), F32))
    carry = lax.fori_loop(0, qi, functools.partial(block, diagonal=False), init)
    _, l, acc = block(qi, carry, True)
    o_ref[...] = (acc / l).astype(BF16)


def _mla_prompt(qcat, kcat, v, *, batch, seq, n_heads, tq, scale, total_rows):
    nq = seq // tq
    dv = v.shape[1] // n_heads
    return pl.pallas_call(
        functools.partial(_flash_kernel, tq=tq, scale=scale),
        grid=(batch, n_heads, nq),
        in_specs=[
            pl.BlockSpec((tq, 2 * LANES), lambda b, h, i: (b * nq + i, h)),
            pl.BlockSpec((seq, 2 * LANES), lambda b, h, i: (b, h)),
            pl.BlockSpec((seq, dv), lambda b, h, i: (b, h)),
        ],
        out_specs=pl.BlockSpec((tq, dv), lambda b, h, i: (b * nq + i, h)),
        out_shape=jax.ShapeDtypeStruct((total_rows, n_heads * dv), BF16),
        compiler_params=_params("parallel", "parallel", "arbitrary"),
    )(qcat, kcat, v)


def _absorb_q_kernel(qcat_ref, wukt_ref, qlat_ref, qr_ref, *, n_heads, kv_rank):
    for h in range(n_heads):
        lo = h * 2 * LANES
        qlat_ref[:, h * kv_rank:(h + 1) * kv_rank] = _dot(qcat_ref[:, lo:lo + LANES], wukt_ref[h]).astype(BF16)
        qr_ref[:, h * LANES:(h + 1) * LANES] = qcat_ref[:, lo + LANES:lo + 2 * LANES]


def _absorb_q(qcat, wukt, *, ts, row_block, n_heads, kv_rank):
    return pl.pallas_call(
        functools.partial(_absorb_q_kernel, n_heads=n_heads, kv_rank=kv_rank),
        grid=(1,),
        in_specs=[
            pl.BlockSpec((ts, qcat.shape[1]), lambda i: (row_block, 0)),
            pl.BlockSpec(wukt.shape, lambda i: (0, 0, 0)),
        ],
        out_specs=[
            pl.BlockSpec((ts, n_heads * kv_rank), lambda i: (0, 0)),
            pl.BlockSpec((ts, n_heads * LANES), lambda i: (0, 0)),
        ],
        out_shape=[
            jax.ShapeDtypeStruct((ts, n_heads * kv_rank), BF16),
            jax.ShapeDtypeStruct((ts, n_heads * LANES), BF16),
        ],
        compiler_params=_params("arbitrary"),
    )(qcat, wukt)


def _decode_kernel(pt_ref, qlat_ref, qr_ref, cnew_ref, krnew_ref, ckv_hbm, krt_hbm, o_ref, cbuf, kbuf, cb_ref, kb_ref,
                   m_ref, l_ref, acc_ref, sems, *, npg, scale, n_heads, rope_dim):
    b = pl.program_id(0)
    step = pl.program_id(1)
    steps = pl.num_programs(1)
    total = pl.num_programs(0) * steps
    n_slots = cbuf.shape[0]
    g_step = b * steps + step
    slot = g_step % n_slots
    page = cbuf.shape[2]

    def fetch(g):
        bb = g // steps
        ss = g % steps
        sl = g % n_slots

        def body(j, carry):
            pg = pt_ref[bb, ss * npg + j]
            pltpu.make_async_copy(ckv_hbm.at[pg], cbuf.at[sl, j], sems.at[0, sl]).start()
            pltpu.make_async_copy(krt_hbm.at[pg], kbuf.at[sl, j], sems.at[1, sl]).start()
            return carry

        lax.fori_loop(0, npg, body, 0, unroll=8)

    @pl.when(g_step == 0)
    def _():
        for g0 in range(n_slots - 1):
            @pl.when(g0 < total)
            def _():
                fetch(jnp.int32(g0))

    @pl.when(g_step + n_slots - 1 < total)
    def _():
        fetch(g_step + n_slots - 1)

    pltpu.make_async_copy(ckv_hbm.at[pl.ds(0, npg)], cbuf.at[slot], sems.at[0, slot]).wait()
    pltpu.make_async_copy(krt_hbm.at[pl.ds(0, npg)], kbuf.at[slot], sems.at[1, slot]).wait()

    @pl.when(step == 0)
    def _():
        m_ref[...] = jnp.full(m_ref.shape, NEG_BIG, F32)
        l_ref[...] = jnp.zeros(l_ref.shape, F32)
        acc_ref[...] = jnp.zeros(acc_ref.shape, F32)

    ql = qlat_ref[0]
    qr = qr_ref[0][:, :rope_dim]

    def update(s, vals):
        m_prev = m_ref[...]
        m_new = jnp.maximum(m_prev, jnp.max(s, axis=-1, keepdims=True))
        alpha = jnp.exp(m_prev - m_new)
        p = jnp.exp(s - m_new)
        l_ref[...] = alpha * l_ref[...] + jnp.sum(p, axis=-1, keepdims=True)
        acc_ref[...] = alpha * acc_ref[...] + _dot(p.astype(BF16), vals)
        m_ref[...] = m_new

    for j in range(npg):
        cb_ref[j * page:(j + 1) * page, :] = cbuf[slot, j].astype(BF16)
        kb_ref[:, j * page:(j + 1) * page] = kbuf[slot, j].astype(BF16)
    cb = cb_ref[...]
    update((_dot_nt(ql, cb) + _dot(qr, kb_ref[...])) * scale, cb)

    @pl.when(step == pl.num_programs(1) - 1)
    def _():
        cn = cnew_ref[0]
        krn = krnew_ref[0]
        s = (_dot_nt(ql, cn) + _dot_nt(qr, krn)) * scale
        t_row = lax.broadcasted_iota(jnp.int32, s.shape, 0) >> (n_heads.bit_length() - 1)
        col = lax.broadcasted_iota(jnp.int32, s.shape, 1)
        update(jnp.where(col <= t_row, s, NEG_BIG), cn)
        o_ref[0] = (acc_ref[...] / l_ref[...]).astype(BF16)


def _mla_decode(page_table, qlat, qr, cnew, krnew, cache_ckv, cache_krope_t, *, npg, scale, n_heads, rope_dim):
    nb, rows, kv_rank = qlat.shape
    n_pages = page_table.shape[1]
    page = cache_ckv.shape[1]
    steps = n_pages // npg
    per_b = lambda b, p, pt: (b, 0, 0)
    grid_spec = pltpu.PrefetchScalarGridSpec(
        num_scalar_prefetch=1,
        grid=(nb, steps),
        in_specs=[
            pl.BlockSpec((1, rows, kv_rank), per_b),
            pl.BlockSpec((1, rows, LANES), per_b),
            pl.BlockSpec((1,) + cnew.shape[1:], per_b),
            pl.BlockSpec((1,) + krnew.shape[1:], per_b),
            pl.BlockSpec(memory_space=pl.ANY),
            pl.BlockSpec(memory_space=pl.ANY),
        ],
        out_specs=pl.BlockSpec((1, rows, kv_rank), per_b),
        scratch_shapes=[
            pltpu.VMEM((DECODE_SLOTS, npg, page, kv_rank), F32),
            pltpu.VMEM((DECODE_SLOTS, npg, rope_dim, page), F32),
            pltpu.VMEM((npg * page, kv_rank), BF16),
            pltpu.VMEM((rope_dim, npg * page), BF16),
            pltpu.VMEM((rows, 1), F32),
            pltpu.VMEM((rows, 1), F32),
            pltpu.VMEM((rows, kv_rank), F32),
            pltpu.SemaphoreType.DMA((2, DECODE_SLOTS)),
        ],
    )
    return pl.pallas_call(
        functools.partial(_decode_kernel, npg=npg, scale=scale, n_heads=n_heads, rope_dim=rope_dim),
        grid_spec=grid_spec,
        out_shape=jax.ShapeDtypeStruct((nb, rows, kv_rank), BF16),
        compiler_params=_params("arbitrary", "arbitrary"),
    )(page_table, qlat, qr, cnew, krnew, cache_ckv, cache_krope_t)


def _head_proj_kernel(x_ref, w_ref, o_ref, *, n_heads):
    k = x_ref.shape[1] // n_heads
    n = o_ref.shape[1] // n_heads
    for h in range(n_heads):
        o_ref[:, h * n:(h + 1) * n] = _dot(x_ref[:, h * k:(h + 1) * k], w_ref[h]).astype(BF16)


def _head_proj(x, w, *, n_heads):
    rows = x.shape[0]
    n = w.shape[2]
    return pl.pallas_call(
        functools.partial(_head_proj_kernel, n_heads=n_heads),
        grid=(1,),
        in_specs=[pl.BlockSpec(x.shape, lambda i: (0, 0)), pl.BlockSpec(w.shape, lambda i: (0, 0, 0))],
        out_specs=pl.BlockSpec((rows, n_heads * n), lambda i: (0, 0)),
        out_shape=jax.ShapeDtypeStruct((rows, n_heads * n), BF16),
        compiler_params=_params("arbitrary"),
    )(x, w)


def _group_norm_gate(o, rg, gn_w):
    mu = jnp.mean(o, axis=-1, keepdims=True)
    d = o - mu
    var = jnp.mean(d * d, axis=-1, keepdims=True)
    on = d * lax.rsqrt(var + GN_EPS) * gn_w
    return (on * (rg * jax.nn.sigmoid(rg))).astype(BF16)


def _ret_prompt_kernel(q_ref, k_ref, v_ref, g_ref, cos_ref, sin_ref, lg_ref, gn_ref, o_ref, st_ref, state,
                       *, n_sub, chunk, k_scale):
    ci = pl.program_id(2)

    @pl.when(ci == 0)
    def _():
        state[...] = jnp.zeros(state.shape, F32)

    lg = lg_ref[0][:, :1]
    ii = lax.broadcasted_iota(jnp.int32, (chunk, chunk), 0)
    jj = lax.broadcasted_iota(jnp.int32, (chunk, chunk), 1)
    diff = (ii - jj).astype(F32)
    decay = jnp.where(diff >= 0, jnp.exp(jnp.maximum(diff, 0.0) * lg), 0.0)
    ri = lax.broadcasted_iota(jnp.int32, (chunk, 1), 0).astype(F32)
    xi = jnp.exp((ri + 1.0) * lg)
    to_end = jnp.exp((chunk - 1.0 - ri) * lg)
    g_chunk = jnp.exp(chunk * lg)
    gn_w = gn_ref[...]
    for sub in range(n_sub):
        sl = pl.ds(sub * chunk, chunk)
        cos = cos_ref[sl, :]
        sin = sin_ref[sl, :]
        q = _rope128(q_ref[sl, :], cos, sin)
        k = _rope128(k_ref[sl, :], cos, sin) * k_scale
        qb = q.astype(BF16)
        vb = v_ref[sl, :].astype(BF16)
        s = _dot_nt(qb, k.astype(BF16)) * decay
        st = state[...]
        o = _dot(s.astype(BF16), vb) + _dot(qb, st.astype(BF16)) * xi
        kw_t = jnp.transpose(k * to_end).astype(BF16)
        state[...] = g_chunk * st + _dot(kw_t, vb)
        o_ref[sl, :] = _group_norm_gate(o, g_ref[sl, :], gn_w)

    @pl.when(ci == pl.num_programs(2) - 1)
    def _():
        st_ref[0, 0] = state[...]


def _ret_prompt(z, cos128, sin128, lgam, gn_w, *, batch, seq, n_heads, dk, dv, rows, offs, total_rows):
    rq_off, rk_off, rv_off, rg_off = offs
    nr = seq // rows
    qk_spec = lambda off: pl.BlockSpec((rows, dk), lambda b, h, c: (b * nr + c, off // dk + h))
    v_spec = lambda off: pl.BlockSpec((rows, dv), lambda b, h, c: (b * nr + c, off // dv + h))
    tab = pl.BlockSpec((rows, LANES), lambda b, h, c: (b * nr + c, 0))
    return pl.pallas_call(
        functools.partial(_ret_prompt_kernel, n_sub=rows // RET_CHUNK, chunk=RET_CHUNK, k_scale=dk ** -0.5),
        grid=(batch, n_heads, nr),
        in_specs=[
            qk_spec(rq_off), qk_spec(rk_off), v_spec(rv_off), v_spec(rg_off), tab, tab,
            pl.BlockSpec((1, 1, LANES), lambda b, h, c: (h, 0, 0)),
            pl.BlockSpec((1, dv), lambda b, h, c: (0, h)),
        ],
        out_specs=[
            pl.BlockSpec((rows, dv), lambda b, h, c: (b * nr + c, h)),
            pl.BlockSpec((1, 1, dk, dv), lambda b, h, c: (b, h, 0, 0)),
        ],
        out_shape=[
            jax.ShapeDtypeStruct((total_rows, n_heads * dv), BF16),
            jax.ShapeDtypeStruct((batch, n_heads, dk, dv), F32),
        ],
        scratch_shapes=[pltpu.VMEM((dk, dv), F32)],
        compiler_params=_params("parallel", "parallel", "arbitrary"),
    )(z, z, z, z, cos128, sin128, lgam, gn_w)


def _ret_sample_kernel(q_ref, k_ref, v_ref, g_ref, cos_ref, sin_ref, lg_ref, gn_ref, st_in_ref, o_ref, st_out_ref,
                       *, nb, t_log2, k_scale):
    ts = 1 << t_log2
    rows = nb * ts
    lg = lg_ref[0][:, :1]
    ii = lax.broadcasted_iota(jnp.int32, (rows, rows), 0)
    jj = lax.broadcasted_iota(jnp.int32, (rows, rows), 1)
    diff = (ii - jj).astype(F32)
    keep = ((ii >> t_log2) == (jj >> t_log2)) & (ii >= jj)
    decay = jnp.where(keep, jnp.exp(jnp.maximum(diff, 0.0) * lg), 0.0)
    ri = lax.broadcasted_iota(jnp.int32, (rows, 1), 0)
    rb = ri >> t_log2
    ti = (ri & (ts - 1)).astype(F32)
    xi = jnp.exp((ti + 1.0) * lg)
    to_end = jnp.exp((ts - 1.0 - ti) * lg)
    g_chunk = jnp.exp(ts * lg)
    cos = cos_ref[...]
    sin = sin_ref[...]
    q = _rope128(q_ref[...], cos, sin)
    k = _rope128(k_ref[...], cos, sin) * k_scale
    qb = q.astype(BF16)
    vb = v_ref[...].astype(BF16)
    s = _dot_nt(qb, k.astype(BF16)) * decay
    o_in = _dot(s.astype(BF16), vb)
    kw = k * to_end
    o_cross = jnp.zeros(o_in.shape, F32)
    for b in range(nb):
        st = st_in_ref[b, 0]
        mine = rb == b
        o_cross = jnp.where(mine, _dot(qb, st.astype(BF16)), o_cross)
        kw_t = jnp.transpose(jnp.where(mine, kw, 0.0)).astype(BF16)
        st_out_ref[b, 0] = g_chunk * st + _dot(kw_t, vb)
    o_ref[...] = _group_norm_gate(o_in + o_cross * xi, g_ref[...], gn_ref[...])


def _ret_sample(z, cos128, sin128, lgam, gn_w, state, *, row0, dec_batch, dec_seq, n_heads, dk, dv, offs):
    rq_off, rk_off, rv_off, rg_off = offs
    t_log2 = dec_seq.bit_length() - 1
    assert 1 << t_log2 == dec_seq
    rows = LANES
    nb = rows // dec_seq
    nblk = dec_batch // nb
    rb0 = row0 // rows
    qk_spec = lambda off: pl.BlockSpec((rows, dk), lambda i, h: (rb0 + i, off // dk + h))
    v_spec = lambda off: pl.BlockSpec((rows, dv), lambda i, h: (rb0 + i, off // dv + h))
    tab = pl.BlockSpec((rows, LANES), lambda i, h: (rb0 + i, 0))
    return pl.pallas_call(
        functools.partial(_ret_sample_kernel, nb=nb, t_log2=t_log2, k_scale=dk ** -0.5),
        grid=(nblk, n_heads),
        in_specs=[
            qk_spec(rq_off), qk_spec(rk_off), v_spec(rv_off), v_spec(rg_off), tab, tab,
            pl.BlockSpec((1, 1, LANES), lambda i, h: (h, 0, 0)),
            pl.BlockSpec((1, dv), lambda i, h: (0, h)),
            pl.BlockSpec((nb, 1, dk, dv), lambda i, h: (i, h, 0, 0)),
        ],
        out_specs=[
            pl.BlockSpec((rows, dv), lambda i, h: (i, h)),
            pl.BlockSpec((nb, 1, dk, dv), lambda i, h: (i, h, 0, 0)),
        ],
        out_shape=[
            jax.ShapeDtypeStruct((dec_batch * dec_seq, n_heads * dv), BF16),
            jax.ShapeDtypeStruct(state.shape, F32),
        ],
        compiler_params=_params("parallel", "parallel"),
    )(z, z, z, z, cos128, sin128, lgam, gn_w, state)


def _softmax_rows(s):
    e = jnp.exp(s - jnp.max(s, axis=-1, keepdims=True))
    return e / jnp.sum(e, axis=-1, keepdims=True)


def _mem_attn_prompt_kernel(q_ref, k_ref, v_ref, o_ref, *, scale):
    s = _dot_nt(q_ref[...].astype(BF16), k_ref[...].astype(BF16)) * scale
    o_ref[...] = _dot(_softmax_rows(s).astype(BF16), v_ref[...].astype(BF16)).astype(BF16)


def _mem_attn_prompt(z, kv, *, batch, seq, mem_tokens, n_heads, hd, tq, mq_off, total_rows):
    nq = seq // tq
    return pl.pallas_call(
        functools.partial(_mem_attn_prompt_kernel, scale=hd ** -0.5),
        grid=(batch, n_heads, nq),
        in_specs=[
            pl.BlockSpec((tq, hd), lambda b, h, i: (b * nq + i, mq_off // hd + h)),
            pl.BlockSpec((mem_tokens, hd), lambda b, h, i: (b, h)),
            pl.BlockSpec((mem_tokens, hd), lambda b, h, i: (b, n_heads + h)),
        ],
        out_specs=pl.BlockSpec((tq, hd), lambda b, h, i: (b * nq + i, h)),
        out_shape=jax.ShapeDtypeStruct((total_rows, n_heads * hd), BF16),
        compiler_params=_params("parallel", "parallel", "parallel"),
    )(z, kv, kv)


def _mem_attn_sample_kernel(q_ref, k_ref, v_ref, o_ref, *, nb, n_heads, scale):
    rows = q_ref.shape[1]
    cols = k_ref.shape[1] * n_heads
    hd = k_ref.shape[3]
    row_h = lax.broadcasted_iota(jnp.int32, (rows, cols), 0) & (n_heads - 1)
    col_h = lax.broadcasted_iota(jnp.int32, (rows, cols), 1) & (n_heads - 1)
    own = row_h == col_h
    for b in range(nb):
        k = k_ref[b].reshape(cols, hd).astype(BF16)
        v = v_ref[b].reshape(cols, hd).astype(BF16)
        s = jnp.where(own, _dot_nt(q_ref[b].astype(BF16), k) * scale, NEG_BIG)
        o_ref[b] = _dot(_softmax_rows(s).astype(BF16), v).astype(BF16)


def _mem_attn_sample(mq, mem_k, mem_v, *, nb):
    dec_batch, rows, hd = mq.shape
    _, mem_tokens, n_heads, _ = mem_k.shape
    assert n_heads & (n_heads - 1) == 0
    cache = pl.BlockSpec((nb, mem_tokens, n_heads, hd), lambda i: (i, 0, 0, 0))
    return pl.pallas_call(
        functools.partial(_mem_attn_sample_kernel, nb=nb, n_heads=n_heads, scale=hd ** -0.5),
        grid=(dec_batch // nb,),
        in_specs=[pl.BlockSpec((nb, rows, hd), lambda i: (i, 0, 0)), cache, cache],
        out_specs=pl.BlockSpec((nb, rows, hd), lambda i: (i, 0, 0)),
        out_shape=jax.ShapeDtypeStruct((dec_batch, rows, hd), BF16),
        compiler_params=_params("parallel"),
    )(mq, mem_k, mem_v)


def _merge_kernel(oap_ref, obp_ref, ocp_ref, oas_ref, obs_ref, ocs_ref, wa_ref, wb_ref, wc_ref, ga_ref, gb_ref,
                  gc_ref, m_ref, *, n_first):
    def run(oa_ref, ob_ref, oc_ref):
        merged = jax.nn.sigmoid(ga_ref[...]) * _dot(oa_ref[...], wa_ref[...])
        merged = merged + jax.nn.sigmoid(gb_ref[...]) * _dot(ob_ref[...], wb_ref[...])
        merged = merged + jax.nn.sigmoid(gc_ref[...]) * _dot(oc_ref[...], wc_ref[...])
        m_ref[...] = merged.astype(BF16)

    i = pl.program_id(1)

    @pl.when(i < n_first)
    def _():
        run(oap_ref, obp_ref, ocp_ref)

    @pl.when(i >= n_first)
    def _():
        run(oas_ref, obs_ref, ocs_ref)


def _merge(prompt_outs, sample_outs, w_a, w_b, w_c, z, *, tm, tn, gz_off):
    tp = prompt_outs[0].shape[0]
    t = tp + sample_outs[0].shape[0]
    n_first = tp // tm
    d = w_a.shape[1]
    nj = d // tn
    act_p = lambda a: pl.BlockSpec((tm, a.shape[1]), lambda j, i: (jnp.minimum(i, n_first - 1), 0))
    act_s = lambda a: pl.BlockSpec((tm, a.shape[1]), lambda j, i: (jnp.maximum(i - n_first, 0), 0))
    wsp = lambda w: pl.BlockSpec((w.shape[0], tn), lambda j, i: (0, j))
    gate = lambda br: pl.BlockSpec((tm, tn), lambda j, i: (i, (gz_off + br * d) // tn + j))
    return pl.pallas_call(
        functools.partial(_merge_kernel, n_first=n_first),
        grid=(nj, t // tm),
        in_specs=[act_p(a) for a in prompt_outs] + [act_s(a) for a in sample_outs]
        + [wsp(w_a), wsp(w_b), wsp(w_c), gate(0), gate(1), gate(2)],
        out_specs=pl.BlockSpec((tm, tn), lambda j, i: (i, j)),
        out_shape=jax.ShapeDtypeStruct((t, d), BF16),
        compiler_params=_params("parallel", "parallel"),
    )(*prompt_outs, *sample_outs, w_a, w_b, w_c, z, z, z)


def _out_proj_kernel(x_ref, m_ref, wo_ref, nw_ref, wrh_ref, wrl_ref, br_ref, x1_ref, h_ref, lg_ref):
    x1 = x_ref[...] + _dot(m_ref[...], wo_ref[...])
    x1_ref[...] = x1
    h = _rms(x1, nw_ref[...])
    h_ref[...] = h
    h_hi = h.astype(BF16)
    h_lo = (h - h_hi.astype(F32)).astype(BF16)
    w_hi = wrh_ref[...]
    lg_ref[...] = _dot(h_hi, w_hi) + _dot(h_lo, w_hi) + _dot(h_hi, wrl_ref[...]) + br_ref[...]


def _out_proj(x, merged, w_o, norm_w, w_router_hi, w_router_lo, b_router, *, tm):
    t, d = x.shape
    row = lambda i: (i, 0)
    const = lambda i: (0, 0)
    return pl.pallas_call(
        _out_proj_kernel,
        grid=(t // tm,),
        in_specs=[
            pl.BlockSpec((tm, d), row),
            pl.BlockSpec((tm, d), row),
            pl.BlockSpec((d, d), const),
            pl.BlockSpec((1, d), const),
            pl.BlockSpec((d, LANES), const),
            pl.BlockSpec((d, LANES), const),
            pl.BlockSpec((1, LANES), const),
        ],
        out_specs=[pl.BlockSpec((tm, d), row), pl.BlockSpec((tm, d), row), pl.BlockSpec((tm, LANES), row)],
        out_shape=[
            jax.ShapeDtypeStruct((t, d), F32),
            jax.ShapeDtypeStruct((t, d), F32),
            jax.ShapeDtypeStruct((t, LANES), F32),
        ],
        compiler_params=_params("parallel"),
    )(x, merged, w_o, norm_w, w_router_hi, w_router_lo, b_router)


def _gather_rows(src_hbm, idx_ref, base, buf, sem, n_rows):
    def issue(r, carry):
        pltpu.make_async_copy(src_hbm.at[pl.ds(idx_ref[base + r], 1), :], buf.at[pl.ds(r, 1), :], sem).start()
        return carry

    lax.fori_loop(0, n_rows, issue, 0, unroll=8)


def _wait_rows(src_hbm, buf, sem, n_rows):
    pltpu.make_async_copy(src_hbm.at[pl.ds(0, n_rows), :], buf, sem).wait()


def _start_rows(src_hbm, idx_ref, base, buf, sem, n_rows):
    for r in range(n_rows):
        pltpu.make_async_copy(src_hbm.at[pl.ds(idx_ref[base + r], 1), :], buf.at[pl.ds(r, 1), :], sem).start()


def _experts_kernel(be_ref, nu_ref, tok_ref, h_hbm, wg_ref, wu_ref, wd_ref, y_ref, buf0, buf1, buf2, wg_b, wu_b,
                    wd_b, sems, *, blk):
    i = pl.program_id(0)
    n_used = nu_ref[0]
    bufs = (buf0, buf1, buf2)

    @pl.when(i == 0)
    def _():
        _gather_rows(h_hbm, tok_ref, 0, buf0, sems.at[0], blk)
        _gather_rows(h_hbm, tok_ref, jnp.minimum(1, n_used - 1) * blk, buf1, sems.at[1], blk)

    @pl.when((i < n_used) & ((i == 0) | (be_ref[i] != be_ref[jnp.maximum(i - 1, 0)])))
    def _():
        wg_b[...] = wg_ref[...].astype(BF16)
        wu_b[...] = wu_ref[...].astype(BF16)
        wd_b[...] = wd_ref[...].astype(BF16)

    def run(k):
        cur, mid, nxt = bufs[k], bufs[(k + 1) % 3], bufs[(k + 2) % 3]
        _wait_rows(h_hbm, cur, sems.at[k], blk)
        _start_rows(h_hbm, tok_ref, jnp.minimum(i + 2, n_used - 1) * blk, nxt, sems.at[(k + 2) % 3], blk)
        xb = cur[...].astype(BF16)
        g = _dot(xb, wg_b[...])
        u = _dot(xb, wu_b[...])
        a = (g * jax.nn.sigmoid(g) * u).astype(BF16)
        y_ref[...] = _dot(a, wd_b[...])

        @pl.when(i == n_used - 1)
        def _():
            _wait_rows(h_hbm, mid, sems.at[(k + 1) % 3], blk)
            _wait_rows(h_hbm, nxt, sems.at[(k + 2) % 3], blk)

    for k in range(3):
        @pl.when((i < n_used) & (i % 3 == k))
        def _():
            run(k)

    @pl.when(i >= n_used)
    def _():
        y_ref[...] = jnp.zeros(y_ref.shape, F32)


def _experts(blk_expert, n_used, row_tok, h, w_gate, w_up, w_down, *, n_blk, blk):
    d = h.shape[1]
    de = w_gate.shape[2]
    grid_spec = pltpu.PrefetchScalarGridSpec(
        num_scalar_prefetch=3,
        grid=(n_blk,),
        in_specs=[
            pl.BlockSpec(memory_space=pl.ANY),
            pl.BlockSpec((None, d, de), lambda i, be, nu, tok: (be[i], 0, 0)),
            pl.BlockSpec((None, d, de), lambda i, be, nu, tok: (be[i], 0, 0)),
            pl.BlockSpec((None, de, d), lambda i, be, nu, tok: (be[i], 0, 0)),
        ],
        out_specs=pl.BlockSpec((blk, d), lambda i, be, nu, tok: (i, 0)),
        scratch_shapes=[
            pltpu.VMEM((blk, d), F32),
            pltpu.VMEM((blk, d), F32),
            pltpu.VMEM((blk, d), F32),
            pltpu.VMEM((d, de), BF16),
            pltpu.VMEM((d, de), BF16),
            pltpu.VMEM((de, d), BF16),
            pltpu.SemaphoreType.DMA((3,)),
        ],
    )
    return pl.pallas_call(
        functools.partial(_experts_kernel, blk=blk),
        grid_spec=grid_spec,
        out_shape=jax.ShapeDtypeStruct((n_blk * blk, d), F32),
        compiler_params=_params("arbitrary"),
    )(blk_expert, n_used, row_tok, h, w_gate, w_up, w_down)


def _combine_kernel(d0_ref, d1_ref, y_hbm, x1_ref, g0_ref, g1_ref, nw_ref, op_ref, os_ref, b00, b01, b02, b10, b11,
                    b12, sems, *, tm, n_first):
    i = pl.program_id(0)
    last = pl.num_programs(0) - 1
    bufs0 = (b00, b01, b02)
    bufs1 = (b10, b11, b12)

    def start(step, k, straight):
        issue = _start_rows if straight else _gather_rows
        issue(y_hbm, d0_ref, step * tm, bufs0[k], sems.at[0, k], tm)
        issue(y_hbm, d1_ref, step * tm, bufs1[k], sems.at[1, k], tm)

    def wait(k):
        _wait_rows(y_hbm, bufs0[k], sems.at[0, k], tm)
        _wait_rows(y_hbm, bufs1[k], sems.at[1, k], tm)

    @pl.when(i == 0)
    def _():
        start(0, 0, False)
        start(jnp.minimum(1, last), 1, False)

    def run(k):
        wait(k)
        start(jnp.minimum(i + 2, last), (k + 2) % 3, True)
        reps = x1_ref.shape[1] // LANES
        g0 = jnp.tile(g0_ref[...], (1, reps))
        g1 = jnp.tile(g1_ref[...], (1, reps))
        x2 = x1_ref[...] + (bufs0[k][...] * g0 + bufs1[k][...] * g1)
        y = _rms(x2, nw_ref[...])

        @pl.when(i < n_first)
        def _():
            op_ref[...] = y

        @pl.when(i >= n_first)
        def _():
            os_ref[...] = y

        @pl.when(i == last)
        def _():
            wait((k + 1) % 3)
            wait((k + 2) % 3)

    for k in range(3):
        @pl.when(i % 3 == k)
        def _():
            run(k)


def _combine(d0, d1, y_rows, x1, g0, g1, norm_w, *, tm, rows_first):
    t, d = x1.shape
    n_first = rows_first // tm
    row = lambda i, a, b: (i, 0)
    grid_spec = pltpu.PrefetchScalarGridSpec(
        num_scalar_prefetch=2,
        grid=(t // tm,),
        in_specs=[
            pl.BlockSpec(memory_space=pl.ANY),
            pl.BlockSpec((tm, d), row),
            pl.BlockSpec((tm, LANES), row),
            pl.BlockSpec((tm, LANES), row),
            pl.BlockSpec((1, d), lambda i, a, b: (0, 0)),
        ],
        out_specs=[
            pl.BlockSpec((tm, d), lambda i, a, b: (jnp.minimum(i, n_first - 1), 0)),
            pl.BlockSpec((tm, d), lambda i, a, b: (jnp.maximum(i - n_first, 0), 0)),
        ],
        scratch_shapes=[pltpu.VMEM((tm, d), F32)] * 6 + [pltpu.SemaphoreType.DMA((2, 3))],
    )
    return pl.pallas_call(
        functools.partial(_combine_kernel, tm=tm, n_first=n_first),
        grid_spec=grid_spec,
        out_shape=[jax.ShapeDtypeStruct((rows_first, d), F32), jax.ShapeDtypeStruct((t - rows_first, d), F32)],
        compiler_params=_params("arbitrary"),
    )(d0, d1, y_rows, x1, g0, g1, norm_w)


def _route(logits, n_groups, per_group, top_k, blk):
    t = logits.shape[0]
    n_exp = n_groups * per_group
    assert top_k == 2
    grp_p = jax.nn.softmax(logits[:, :n_groups], axis=-1)
    g_idx = jnp.argmax(grp_p, axis=-1, keepdims=True)
    g_prob = jnp.max(grp_p, axis=-1, keepdims=True)
    e_logit = logits[:, n_groups:n_groups + n_exp].reshape(t, n_groups, per_group)
    in_grp = jnp.take_along_axis(e_logit, g_idx[:, :, None], axis=1)[:, 0, :]
    i1 = jnp.argmax(in_grp, axis=-1, keepdims=True)
    rest = jnp.where(jnp.arange(per_group)[None, :] == i1, -jnp.inf, in_grp)
    i2 = jnp.argmax(rest, axis=-1, keepdims=True)
    top_i = jnp.concatenate([i1, i2], axis=-1)
    top_v = jnp.concatenate([jnp.max(in_grp, axis=-1, keepdims=True), jnp.max(rest, axis=-1, keepdims=True)], axis=-1)
    gate = g_prob * jax.nn.softmax(top_v, axis=-1)
    expert = (g_idx * per_group + top_i).reshape(-1).astype(jnp.int32)
    a = t * top_k
    onehot = expert[:, None] == jnp.arange(n_exp, dtype=jnp.int32)[None, :]
    csum = jnp.cumsum(onehot.astype(jnp.int32), axis=0)
    counts = csum[-1]
    rank = jnp.sum(jnp.where(onehot, csum - 1, 0), axis=1)
    padded = (counts + blk - 1) // blk * blk
    pad_end = jnp.cumsum(padded)
    pad_start = pad_end - padded
    dest = (pad_start[expert] + rank).astype(jnp.int32)
    n_blk = (a + n_exp * (blk - 1) + blk - 1) // blk
    tok = jnp.arange(a, dtype=jnp.int32) // top_k
    row_tok = jnp.zeros((n_blk * blk,), jnp.int32).at[dest].set(tok)
    blk_start = jnp.arange(n_blk, dtype=jnp.int32) * blk
    blk_expert = jnp.minimum(jnp.sum(pad_end[None, :] <= blk_start[:, None], axis=1), n_exp - 1).astype(jnp.int32)
    n_used = (pad_end[-1:] // blk).astype(jnp.int32)
    return gate, dest.reshape(t, top_k), row_tok, blk_expert, n_used, n_blk


def _pick(n, *prefs):
    for p in prefs:
        if n % p == 0:
            return p
    return n


def kernel(x_prompt, x_sample, mem_prompt, cache_ckv, cache_krope, state_ret, cache_mem_k, cache_mem_v, page_table, attn_norm_w, w_in, mla_q_norm_w, mla_w_uq, mla_kv_norm_w, mla_w_uk, mla_w_uv, ret_gn_w, mem_norm_w, mem_w_kv, w_branch_a, w_branch_b, w_branch_c, w_out, ffn_norm_w, router_grp_w, router_grp_b, router_exp_w, router_exp_b, exp_w_gate, exp_w_up, exp_w_down, final_norm_w):
    batch, seq, d = x_prompt.shape
    dec_batch, dec_seq, _ = x_sample.shape
    depth = w_in.shape[0]
    q_rank, n_heads, qh = mla_w_uq.shape[1:]
    kv_rank, _, nope = mla_w_uk.shape[1:]
    rope_dim = qh - nope
    v_dim = mla_w_uv.shape[3]
    ret_heads, ret_dv = ret_gn_w.shape[1:]
    ret_dk = state_ret.shape[3]
    mem_tokens = mem_prompt.shape[1]
    mem_heads, mem_hd = cache_mem_k.shape[3:]
    mem_width = mem_heads * mem_hd
    n_groups = router_grp_w.shape[2]
    n_exp = router_exp_w.shape[2]
    per_group = n_exp // n_groups
    top_k = 2
    page = cache_ckv.shape[2]
    past_len = page_table.shape[1] * page
    assert nope == LANES and v_dim == LANES and rope_dim == 64 and ret_dk == LANES
    assert q_rank == kv_rank and n_groups + n_exp <= LANES

    tp = batch * seq
    ts = dec_batch * dec_seq
    t = tp + ts
    ret_qk = ret_heads * ret_dk
    ret_v = ret_heads * ret_dv
    cq_off, ckv_off = 0, q_rank
    rq_off = ckv_off + kv_rank
    rk_off = rq_off + ret_qk
    rv_off = rk_off + ret_qk
    rg_off = rv_off + ret_v
    mq_off = rg_off + ret_v
    gz_off = mq_off + mem_width
    ret_offs = (rq_off, rk_off, rv_off, rg_off)

    pos = jnp.concatenate([jnp.tile(jnp.arange(seq), batch), jnp.tile(past_len + jnp.arange(dec_seq), dec_batch)])
    cos64, sin64 = _rope_tables(pos, rope_dim)
    cos128, sin128 = _rope_tables(pos, ret_dk)
    lgam = jnp.log1p(-jnp.exp2(-5.0 - jnp.arange(ret_heads, dtype=F32)))
    lgam = jnp.broadcast_to(lgam[:, None, None], (ret_heads, 1, LANES))

    xp = x_prompt.reshape(tp, d)
    xs = x_sample.reshape(ts, d)
    x = jnp.concatenate([xp, xs], axis=0)

    assert depth == 1, "the fused final norm assumes a single layer"
    l = 0
    tm_mid = _pick(t, 256)

    wi = w_in[l]
    kr_lo = q_rank + kv_rank
    w_main = jnp.concatenate([wi[:, :kr_lo], wi[:, kr_lo + rope_dim:]], axis=1).astype(BF16)
    w_kr = jnp.pad(wi[:, kr_lo:kr_lo + rope_dim], ((0, 0), (0, LANES - rope_dim))).astype(BF16)
    z, zk = _norm_proj(x, attn_norm_w[l], w_main, w_kr, tm=_pick(t, 1536, 1024, 512),
                       tn=_pick(w_main.shape[1], 512))

    wuq = mla_w_uq[l]
    wqn = wuq[:, :, :nope].reshape(q_rank, n_heads * LANES).astype(BF16)
    wqr = jnp.pad(wuq[:, :, nope:], ((0, 0), (0, 0), (0, LANES - rope_dim))).reshape(q_rank, n_heads * LANES).astype(BF16)
    wuk = mla_w_uk[l].reshape(kv_rank, n_heads * nope).astype(BF16)
    wuv = mla_w_uv[l].reshape(kv_rank, n_heads * v_dim).astype(BF16)
    qcat, c, kr, kcat, v = _mla_prep(
        z, zk, cos64, sin64, mla_q_norm_w[l].reshape(1, -1), mla_kv_norm_w[l].reshape(1, -1), wqn, wqr, wuk, wuv,
        tm=tm_mid, n_heads=n_heads, q_rank=q_rank, kv_rank=kv_rank, rope_dim=rope_dim)

    scale = qh ** -0.5
    oa_p = _mla_prompt(qcat, kcat, v, batch=batch, seq=seq, n_heads=n_heads, tq=_pick(seq, 512), scale=scale,
                       total_rows=tp)
    wukt = jnp.transpose(mla_w_uk[l], (1, 2, 0)).astype(BF16)
    qlat, qr = _absorb_q(qcat, wukt, ts=ts, row_block=tp // ts, n_heads=n_heads, kv_rank=kv_rank)
    rows = dec_seq * n_heads
    new_pad = LANES - dec_seq
    cnew = jnp.pad(c[tp:].reshape(dec_batch, dec_seq, kv_rank), ((0, 0), (0, new_pad), (0, 0))).astype(BF16)
    krnew = jnp.pad(kr[tp:].reshape(dec_batch, dec_seq, rope_dim), ((0, 0), (0, new_pad), (0, 0))).astype(BF16)
    o_lat = _mla_decode(
        page_table, qlat.reshape(dec_batch, rows, kv_rank), qr.reshape(dec_batch, rows, LANES), cnew, krnew,
        cache_ckv[l], jnp.swapaxes(cache_krope[l], 1, 2), npg=_pick(page_table.shape[1], 32), scale=scale,
        n_heads=n_heads, rope_dim=rope_dim)
    wuv_h = jnp.transpose(mla_w_uv[l], (1, 0, 2)).astype(BF16)
    oa_s = _head_proj(o_lat.reshape(ts, n_heads * kv_rank), wuv_h, n_heads=n_heads)

    gn_w = ret_gn_w[l].reshape(1, ret_v)
    ob_p, st_p = _ret_prompt(z, cos128, sin128, lgam, gn_w, batch=batch, seq=seq, n_heads=ret_heads, dk=ret_dk,
                             dv=ret_dv, rows=_pick(seq, 1024, 512), offs=ret_offs, total_rows=tp)
    ob_s, st_s = _ret_sample(z, cos128, sin128, lgam, gn_w, state_ret[l], row0=tp, dec_batch=dec_batch,
                             dec_seq=dec_seq, n_heads=ret_heads, dk=ret_dk, dv=ret_dv, offs=ret_offs)

    mem_rows = batch * mem_tokens
    (kv,) = _norm_proj(mem_prompt.reshape(mem_rows, d), mem_norm_w[l], mem_w_kv[l].astype(BF16),
                       tm=_pick(mem_rows, 1024, 512), tn=_pick(2 * mem_width, 512))
    oc_p = _mem_attn_prompt(z, kv, batch=batch, seq=seq, mem_tokens=mem_tokens, n_heads=mem_heads, hd=mem_hd,
                            tq=_pick(seq, 512), mq_off=mq_off, total_rows=tp)
    mq_s = z[tp:, mq_off:mq_off + mem_width].reshape(dec_batch, dec_seq * mem_heads, mem_hd)
    oc_s = _mem_attn_sample(mq_s, cache_mem_k[l], cache_mem_v[l], nb=4).reshape(ts, mem_width)

    tm_merge = _pick(ts, 512, 256, 128)
    assert tp % tm_merge == 0
    merged = _merge((oa_p, ob_p, oc_p), (oa_s, ob_s, oc_s), w_branch_a[l].astype(BF16), w_branch_b[l].astype(BF16),
                    w_branch_c[l].astype(BF16), z, tm=tm_merge, tn=_pick(d, 1024), gz_off=gz_off)
    n_route = n_groups + n_exp
    w_router = jnp.pad(jnp.concatenate([router_grp_w[l], router_exp_w[l]], axis=1), ((0, 0), (0, LANES - n_route)))
    w_router_hi = w_router.astype(BF16)
    w_router_lo = (w_router - w_router_hi.astype(F32)).astype(BF16)
    b_router = jnp.pad(jnp.concatenate([router_grp_b[l], router_exp_b[l]]), (0, LANES - n_route)).reshape(1, LANES)
    x1, h2, logits = _out_proj(x, merged, w_out[l].astype(BF16), ffn_norm_w[l].reshape(1, d), w_router_hi,
                               w_router_lo, b_router, tm=tm_mid)

    gate, dest, row_tok, blk_expert, n_used, n_blk = _route(logits, n_groups, per_group, top_k, MOE_BLOCK)
    y_rows = _experts(blk_expert, n_used, row_tok, h2, exp_w_gate[l], exp_w_up[l], exp_w_down[l], n_blk=n_blk,
                      blk=MOE_BLOCK)
    g0 = jnp.broadcast_to(gate[:, 0:1], (t, LANES))
    g1 = jnp.broadcast_to(gate[:, 1:2], (t, LANES))
    tm_c = _pick(ts, 128)
    assert tp % tm_c == 0
    y_p, y_s = _combine(dest[:, 0], dest[:, 1], y_rows, x1, g0, g1, final_norm_w.reshape(1, d), tm=tm_c, rows_first=tp)

    lead = lambda a: a[None]
    return (
        y_p.reshape(batch, seq, d),
        y_s.reshape(dec_batch, dec_seq, d),
        lead(c[:tp].reshape(batch, seq, kv_rank)),
        lead(kr[:tp].reshape(batch, seq, rope_dim)),
        lead(st_p),
        lead(kv[:, :mem_width].reshape(batch, mem_tokens, mem_heads, mem_hd)),
        lead(kv[:, mem_width:].reshape(batch, mem_tokens, mem_heads, mem_hd)),
        lead(c[tp:].reshape(dec_batch, dec_seq, kv_rank)),
        lead(kr[tp:].reshape(dec_batch, dec_seq, rope_dim)),
        lead(st_s),
    )
```

```python
import functools

import jax
import jax.numpy as jnp
from jax import lax
from jax.experimental import pallas as pl
from jax.experimental.pallas import tpu as pltpu

F32 = jnp.float32
BF16 = jnp.bfloat16

NORM_EPS = 1e-6
GN_EPS = 1e-5
ROPE_THETA = 10000.0
RET_CHUNK = 128
MOE_BLOCK = 128
DECODE_SLOTS = 3
LANES = 128
NEG_BIG = -1e30
LOG2_E = 1.4426950408889634
VMEM_LIMIT_BYTES = 56 * 1024 * 1024


def _params(*sem):
    return pltpu.CompilerParams(dimension_semantics=sem, vmem_limit_bytes=VMEM_LIMIT_BYTES)


def _dot(a, b):
    return jnp.dot(a, b, preferred_element_type=F32)


def _dot_nt(a, b):
    return lax.dot_general(a, b, (((1,), (1,)), ((), ())), preferred_element_type=F32)


def _rms(x, w):
    return x * lax.rsqrt(jnp.mean(x * x, axis=-1, keepdims=True) + NORM_EPS) * w


def _rope64(x, cos, sin):
    lane = lax.broadcasted_iota(jnp.int32, x.shape, 1)
    first = (lane & 63) < 32
    swapped = jnp.where(first, pltpu.roll(x, 96, 1), pltpu.roll(x, 32, 1))
    return x * cos + swapped * sin


def _rope128(x, cos, sin):
    return x * cos + pltpu.roll(x, 64, 1) * sin


def _rope_tables(pos, dim):
    half = dim // 2
    inv = ROPE_THETA ** (-jnp.arange(half, dtype=F32) / half)
    ang = pos.astype(F32)[:, None] * inv[None, :]
    cos = jnp.cos(ang)
    sin = jnp.sin(ang)
    c = jnp.concatenate([cos, cos], axis=-1)
    s = jnp.concatenate([-sin, sin], axis=-1)
    reps = LANES // dim
    return jnp.tile(c, (1, reps)), jnp.tile(s, (1, reps))


def _norm_proj_kernel(x_ref, nw_ref, w_ref, z_ref, h_ref):
    @pl.when(pl.program_id(1) == 0)
    def _():
        h_ref[...] = _rms(x_ref[...], nw_ref[...]).astype(BF16)

    z_ref[...] = _dot(h_ref[...], w_ref[...])


def _norm_proj_extra_kernel(x_ref, nw_ref, w_ref, we_ref, z_ref, ze_ref, h_ref):
    @pl.when(pl.program_id(1) == 0)
    def _():
        h = _rms(x_ref[...], nw_ref[...]).astype(BF16)
        h_ref[...] = h
        ze_ref[...] = _dot(h, we_ref[...])

    z_ref[...] = _dot(h_ref[...], w_ref[...])


def _norm_proj(x, norm_w, w, w_extra=None, *, tm, tn):
    t, d = x.shape
    n = w.shape[1]
    grid = (t // tm, n // tn)
    in_specs = [
        pl.BlockSpec((tm, d), lambda i, j: (i, 0)),
        pl.BlockSpec((1, d), lambda i, j: (0, 0)),
        pl.BlockSpec((d, tn), lambda i, j: (0, j)),
    ]
    out_specs = [pl.BlockSpec((tm, tn), lambda i, j: (i, j))]
    out_shape = [jax.ShapeDtypeStruct((t, n), F32)]
    args = [x, norm_w.reshape(1, d), w]
    body = _norm_proj_kernel
    if w_extra is not None:
        ne = w_extra.shape[1]
        in_specs.append(pl.BlockSpec((d, ne), lambda i, j: (0, 0)))
        out_specs.append(pl.BlockSpec((tm, ne), lambda i, j: (i, 0)))
        out_shape.append(jax.ShapeDtypeStruct((t, ne), F32))
        args.append(w_extra)
        body = _norm_proj_extra_kernel
    return pl.pallas_call(
        body,
        grid=grid,
        in_specs=in_specs,
        out_specs=out_specs,
        out_shape=out_shape,
        scratch_shapes=[pltpu.VMEM((tm, d), BF16)],
        compiler_params=_params("parallel", "arbitrary"),
    )(*args)


def _mla_prep_kernel(cq_ref, ckv_ref, zk_ref, cos_ref, sin_ref, qnw_ref, kvnw_ref, wqn_ref, wqr_ref,
                     wuk_ref, wuv_ref, qcat_ref, c_ref, kr_ref, kcat_ref, v_ref, *, n_heads, rope_dim):
    cos = cos_ref[...]
    sin = sin_ref[...]
    cqn = _rms(cq_ref[...], qnw_ref[...]).astype(BF16)
    qn = _dot(cqn, wqn_ref[...])
    qr = _dot(cqn, wqr_ref[...])
    c = _rms(ckv_ref[...], kvnw_ref[...])
    c_ref[...] = c
    cb = c.astype(BF16)
    kn = _dot(cb, wuk_ref[...])
    v_ref[...] = _dot(cb, wuv_ref[...]).astype(BF16)
    kr = _rope64(zk_ref[...], cos, sin)
    kr_ref[...] = kr[:, :rope_dim]
    krb = kr.astype(BF16)
    for h in range(n_heads):
        lo = h * 2 * LANES
        sl = slice(h * LANES, (h + 1) * LANES)
        qcat_ref[:, lo:lo + LANES] = qn[:, sl].astype(BF16)
        qcat_ref[:, lo + LANES:lo + 2 * LANES] = _rope64(qr[:, sl], cos, sin).astype(BF16)
        kcat_ref[:, lo:lo + LANES] = kn[:, sl].astype(BF16)
        kcat_ref[:, lo + LANES:lo + 2 * LANES] = krb


def _mla_prep(z, zk, cos64, sin64, qnw, kvnw, wqn, wqr, wuk, wuv, *, tm, n_heads, q_rank, kv_rank, rope_dim):
    t = z.shape[0]
    row = lambda i: (i, 0)
    const = lambda i: (0, 0)
    hw = n_heads * LANES
    return pl.pallas_call(
        functools.partial(_mla_prep_kernel, n_heads=n_heads, rope_dim=rope_dim),
        grid=(t // tm,),
        in_specs=[
            pl.BlockSpec((tm, q_rank), lambda i: (i, 0)),
            pl.BlockSpec((tm, kv_rank), lambda i: (i, q_rank // kv_rank)),
            pl.BlockSpec((tm, LANES), row),
            pl.BlockSpec((tm, LANES), row),
            pl.BlockSpec((tm, LANES), row),
            pl.BlockSpec((1, q_rank), const),
            pl.BlockSpec((1, kv_rank), const),
            pl.BlockSpec((q_rank, hw), const),
            pl.BlockSpec((q_rank, hw), const),
            pl.BlockSpec((kv_rank, hw), const),
            pl.BlockSpec((kv_rank, hw), const),
        ],
        out_specs=[
            pl.BlockSpec((tm, 2 * hw), row),
            pl.BlockSpec((tm, kv_rank), row),
            pl.BlockSpec((tm, rope_dim), row),
            pl.BlockSpec((tm, 2 * hw), row),
            pl.BlockSpec((tm, hw), row),
        ],
        out_shape=[
            jax.ShapeDtypeStruct((t, 2 * hw), BF16),
            jax.ShapeDtypeStruct((t, kv_rank), F32),
            jax.ShapeDtypeStruct((t, rope_dim), F32),
            jax.ShapeDtypeStruct((t, 2 * hw), BF16),
            jax.ShapeDtypeStruct((t, hw), BF16),
        ],
        compiler_params=_params("parallel"),
    )(z, z, zk, cos64, sin64, qnw, kvnw, wqn, wqr, wuk, wuv)


def _flash_kernel(q_ref, k_ref, v_ref, o_ref, *, tq, scale):
    qi = pl.program_id(2)
    q = q_ref[...]

    def block(j, carry, diagonal):
        m, l, acc = carry
        start = pl.multiple_of(j * tq, tq)
        k = k_ref[pl.ds(start, tq), :]
        v = v_ref[pl.ds(start, tq), :]
        s = _dot_nt(q, k) * (scale * LOG2_E)
        if diagonal:
            rows = lax.broadcasted_iota(jnp.int32, (tq, tq), 0)
            cols = lax.broadcasted_iota(jnp.int32, (tq, tq), 1)
            s = jnp.where(cols <= rows, s, NEG_BIG)
        m_new = jnp.maximum(m, jnp.max(s, axis=-1, keepdims=True))
        alpha = jnp.exp2(m - m_new)
        p = jnp.exp2(s - m_new)
        l = alpha * l + jnp.sum(p, axis=-1, keepdims=True)
        acc = alpha * acc + _dot(p.astype(BF16), v)
        return m_new, l, acc

    dv = v_ref.shape[-1]
    init = (jnp.full((tq, 1), NEG_BIG, F32), jnp.zeros((tq, 1), F32), jnp.zeros((tq, dv), F32))
    carry = lax.fori_loop(0, qi, functools.partial(block, diagonal=False), init)
    _, l, acc = block(qi, carry, True)
    o_ref[...] = (acc / l).astype(BF16)


def _mla_prompt(qcat, kcat, v, *, batch, seq, n_heads, tq, scale, total_rows):
    nq = seq // tq
    dv = v.shape[1] // n_heads
    return pl.pallas_call(
        functools.partial(_flash_kernel, tq=tq, scale=scale),
        grid=(batch, n_heads, nq),
        in_specs=[
            pl.BlockSpec((tq, 2 * LANES), lambda b, h, i: (b * nq + i, h)),
            pl.BlockSpec((seq, 2 * LANES), lambda b, h, i: (b, h)),
            pl.BlockSpec((seq, dv), lambda b, h, i: (b, h)),
        ],
        out_specs=pl.BlockSpec((tq, dv), lambda b, h, i: (b * nq + i, h)),
        out_shape=jax.ShapeDtypeStruct((total_rows, n_heads * dv), BF16),
        compiler_params=_params("parallel", "parallel", "arbitrary"),
    )(qcat, kcat, v)


def _absorb_q_kernel(qcat_ref, wukt_ref, qlat_ref, qr_ref, *, n_heads, kv_rank):
    for h in range(n_heads):
        lo = h * 2 * LANES
        qlat_ref[:, h * kv_rank:(h + 1) * kv_rank] = _dot(qcat_ref[:, lo:lo + LANES], wukt_ref[h]).astype(BF16)
        qr_ref[:, h * LANES:(h + 1) * LANES] = qcat_ref[:, lo + LANES:lo + 2 * LANES]


def _absorb_q(qcat, wukt, *, ts, row_block, n_heads, kv_rank):
    return pl.pallas_call(
        functools.partial(_absorb_q_kernel, n_heads=n_heads, kv_rank=kv_rank),
        grid=(1,),
        in_specs=[
            pl.BlockSpec((ts, qcat.shape[1]), lambda i: (row_block, 0)),
            pl.BlockSpec(wukt.shape, lambda i: (0, 0, 0)),
        ],
        out_specs=[
            pl.BlockSpec((ts, n_heads * kv_rank), lambda i: (0, 0)),
            pl.BlockSpec((ts, n_heads * LANES), lambda i: (0, 0)),
        ],
        out_shape=[
            jax.ShapeDtypeStruct((ts, n_heads * kv_rank), BF16),
            jax.ShapeDtypeStruct((ts, n_heads * LANES), BF16),
        ],
        compiler_params=_params("arbitrary"),
    )(qcat, wukt)


def _decode_kernel(pt_ref, qlat_ref, qr_ref, cnew_ref, krnew_ref, ckv_hbm, krt_hbm, o_ref, cbuf, kbuf, cb_ref, kb_ref,
                   m_ref, l_ref, acc_ref, sems, *, npg, scale, n_heads, rope_dim):
    b = pl.program_id(0)
    step = pl.program_id(1)
    steps = pl.num_programs(1)
    total = pl.num_programs(0) * steps
    n_slots = cbuf.shape[0]
    g_step = b * steps + step
    slot = g_step % n_slots
    page = cbuf.shape[2]

    def fetch(g):
        bb = g // steps
        ss = g % steps
        sl = g % n_slots

        def body(j, carry):
            pg = pt_ref[bb, ss * npg + j]
            pltpu.make_async_copy(ckv_hbm.at[pg], cbuf.at[sl, j], sems.at[0, sl]).start()
            pltpu.make_async_copy(krt_hbm.at[pg], kbuf.at[sl, j], sems.at[1, sl]).start()
            return carry

        lax.fori_loop(0, npg, body, 0, unroll=8)

    @pl.when(g_step == 0)
    def _():
        for g0 in range(n_slots - 1):
            @pl.when(g0 < total)
            def _():
                fetch(jnp.int32(g0))

    @pl.when(g_step + n_slots - 1 < total)
    def _():
        fetch(g_step + n_slots - 1)

    pltpu.make_async_copy(ckv_hbm.at[pl.ds(0, npg)], cbuf.at[slot], sems.at[0, slot]).wait()
    pltpu.make_async_copy(krt_hbm.at[pl.ds(0, npg)], kbuf.at[slot], sems.at[1, slot]).wait()

    @pl.when(step == 0)
    def _():
        m_ref[...] = jnp.full(m_ref.shape, NEG_BIG, F32)
        l_ref[...] = jnp.zeros(l_ref.shape, F32)
        acc_ref[...] = jnp.zeros(acc_ref.shape, F32)

    ql = qlat_ref[0]
    qr = qr_ref[0][:, :rope_dim]

    def update(s, vals):
        m_prev = m_ref[...]
        m_new = jnp.maximum(m_prev, jnp.max(s, axis=-1, keepdims=True))
        alpha = jnp.exp(m_prev - m_new)
        p = jnp.exp(s - m_new)
        l_ref[...] = alpha * l_ref[...] + jnp.sum(p, axis=-1, keepdims=True)
        acc_ref[...] = alpha * acc_ref[...] + _dot(p.astype(BF16), vals)
        m_ref[...] = m_new

    for j in range(npg):
        cb_ref[j * page:(j + 1) * page, :] = cbuf[slot, j].astype(BF16)
        kb_ref[:, j * page:(j + 1) * page] = kbuf[slot, j].astype(BF16)
    cb = cb_ref[...]
    update((_dot_nt(ql, cb) + _dot(qr, kb_ref[...])) * scale, cb)

    @pl.when(step == pl.num_programs(1) - 1)
    def _():
        cn = cnew_ref[0]
        krn = krnew_ref[0]
        s = (_dot_nt(ql, cn) + _dot_nt(qr, krn)) * scale
        t_row = lax.broadcasted_iota(jnp.int32, s.shape, 0) >> (n_heads.bit_length() - 1)
        col = lax.broadcasted_iota(jnp.int32, s.shape, 1)
        update(jnp.where(col <= t_row, s, NEG_BIG), cn)
        o_ref[0] = (acc_ref[...] / l_ref[...]).astype(BF16)


def _mla_decode(page_table, qlat, qr, cnew, krnew, cache_ckv, cache_krope_t, *, npg, scale, n_heads, rope_dim):
    nb, rows, kv_rank = qlat.shape
    n_pages = page_table.shape[1]
    page = cache_ckv.shape[1]
    steps = n_pages // npg
    per_b = lambda b, p, pt: (b, 0, 0)
    grid_spec = pltpu.PrefetchScalarGridSpec(
        num_scalar_prefetch=1,
        grid=(nb, steps),
        in_specs=[
            pl.BlockSpec((1, rows, kv_rank), per_b),
            pl.BlockSpec((1, rows, LANES), per_b),
            pl.BlockSpec((1,) + cnew.shape[1:], per_b),
            pl.BlockSpec((1,) + krnew.shape[1:], per_b),
            pl.BlockSpec(memory_space=pl.ANY),
            pl.BlockSpec(memory_space=pl.ANY),
        ],
        out_specs=pl.BlockSpec((1, rows, kv_rank), per_b),
        scratch_shapes=[
            pltpu.VMEM((DECODE_SLOTS, npg, page, kv_rank), F32),
            pltpu.VMEM((DECODE_SLOTS, npg, rope_dim, page), F32),
            pltpu.VMEM((npg * page, kv_rank), BF16),
            pltpu.VMEM((rope_dim, npg * page), BF16),
            pltpu.VMEM((rows, 1), F32),
            pltpu.VMEM((rows, 1), F32),
            pltpu.VMEM((rows, kv_rank), F32),
            pltpu.SemaphoreType.DMA((2, DECODE_SLOTS)),
        ],
    )
    return pl.pallas_call(
        functools.partial(_decode_kernel, npg=npg, scale=scale, n_heads=n_heads, rope_dim=rope_dim),
        grid_spec=grid_spec,
        out_shape=jax.ShapeDtypeStruct((nb, rows, kv_rank), BF16),
        compiler_params=_params("arbitrary", "arbitrary"),
    )(page_table, qlat, qr, cnew, krnew, cache_ckv, cache_krope_t)


def _head_proj_kernel(x_ref, w_ref, o_ref, *, n_heads):
    k = x_ref.shape[1] // n_heads
    n = o_ref.shape[1] // n_heads
    for h in range(n_heads):
        o_ref[:, h * n:(h + 1) * n] = _dot(x_ref[:, h * k:(h + 1) * k], w_ref[h]).astype(BF16)


def _head_proj(x, w, *, n_heads):
    rows = x.shape[0]
    n = w.shape[2]
    return pl.pallas_call(
        functools.partial(_head_proj_kernel, n_heads=n_heads),
        grid=(1,),
        in_specs=[pl.BlockSpec(x.shape, lambda i: (0, 0)), pl.BlockSpec(w.shape, lambda i: (0, 0, 0))],
        out_specs=pl.BlockSpec((rows, n_heads * n), lambda i: (0, 0)),
        out_shape=jax.ShapeDtypeStruct((rows, n_heads * n), BF16),
        compiler_params=_params("arbitrary"),
    )(x, w)


def _group_norm_gate(o, rg, gn_w):
    mu = jnp.mean(o, axis=-1, keepdims=True)
    d = o - mu
    var = jnp.mean(d * d, axis=-1, keepdims=True)
    on = d * lax.rsqrt(var + GN_EPS) * gn_w
    return (on * (rg * jax.nn.sigmoid(rg))).astype(BF16)


def _ret_prompt_kernel(q_ref, k_ref, v_ref, g_ref, cos_ref, sin_ref, lg_ref, gn_ref, o_ref, st_ref, state,
                       *, n_sub, chunk, k_scale):
    ci = pl.program_id(2)

    @pl.when(ci == 0)
    def _():
        state[...] = jnp.zeros(state.shape, F32)

    lg = lg_ref[0][:, :1]
    ii = lax.broadcasted_iota(jnp.int32, (chunk, chunk), 0)
    jj = lax.broadcasted_iota(jnp.int32, (chunk, chunk), 1)
    diff = (ii - jj).astype(F32)
    decay = jnp.where(diff >= 0, jnp.exp(jnp.maximum(diff, 0.0) * lg), 0.0)
    ri = lax.broadcasted_iota(jnp.int32, (chunk, 1), 0).astype(F32)
    xi = jnp.exp((ri + 1.0) * lg)
    to_end = jnp.exp((chunk - 1.0 - ri) * lg)
    g_chunk = jnp.exp(chunk * lg)
    gn_w = gn_ref[...]
    for sub in range(n_sub):
        sl = pl.ds(sub * chunk, chunk)
        cos = cos_ref[sl, :]
        sin = sin_ref[sl, :]
        q = _rope128(q_ref[sl, :], cos, sin)
        k = _rope128(k_ref[sl, :], cos, sin) * k_scale
        qb = q.astype(BF16)
        vb = v_ref[sl, :].astype(BF16)
        s = _dot_nt(qb, k.astype(BF16)) * decay
        st = state[...]
        o = _dot(s.astype(BF16), vb) + _dot(qb, st.astype(BF16)) * xi
        kw_t = jnp.transpose(k * to_end).astype(BF16)
        state[...] = g_chunk * st + _dot(kw_t, vb)
        o_ref[sl, :] = _group_norm_gate(o, g_ref[sl, :], gn_w)

    @pl.when(ci == pl.num_programs(2) - 1)
    def _():
        st_ref[0, 0] = state[...]


def _ret_prompt(z, cos128, sin128, lgam, gn_w, *, batch, seq, n_heads, dk, dv, rows, offs, total_rows):
    rq_off, rk_off, rv_off, rg_off = offs
    nr = seq // rows
    qk_spec = lambda off: pl.BlockSpec((rows, dk), lambda b, h, c: (b * nr + c, off // dk + h))
    v_spec = lambda off: pl.BlockSpec((rows, dv), lambda b, h, c: (b * nr + c, off // dv + h))
    tab = pl.BlockSpec((rows, LANES), lambda b, h, c: (b * nr + c, 0))
    return pl.pallas_call(
        functools.partial(_ret_prompt_kernel, n_sub=rows // RET_CHUNK, chunk=RET_CHUNK, k_scale=dk ** -0.5),
        grid=(batch, n_heads, nr),
        in_specs=[
            qk_spec(rq_off), qk_spec(rk_off), v_spec(rv_off), v_spec(rg_off), tab, tab,
            pl.BlockSpec((1, 1, LANES), lambda b, h, c: (h, 0, 0)),
            pl.BlockSpec((1, dv), lambda b, h, c: (0, h)),
        ],
        out_specs=[
            pl.BlockSpec((rows, dv), lambda b, h, c: (b * nr + c, h)),
            pl.BlockSpec((1, 1, dk, dv), lambda b, h, c: (b, h, 0, 0)),
        ],
        out_shape=[
            jax.ShapeDtypeStruct((total_rows, n_heads * dv), BF16),
            jax.ShapeDtypeStruct((batch, n_heads, dk, dv), F32),
        ],
        scratch_shapes=[pltpu.VMEM((dk, dv), F32)],
        compiler_params=_params("parallel", "parallel", "arbitrary"),
    )(z, z, z, z, cos128, sin128, lgam, gn_w)


def _ret_sample_kernel(q_ref, k_ref, v_ref, g_ref, cos_ref, sin_ref, lg_ref, gn_ref, st_in_ref, o_ref, st_out_ref,
                       *, nb, t_log2, k_scale):
    ts = 1 << t_log2
    rows = nb * ts
    lg = lg_ref[0][:, :1]
    ii = lax.broadcasted_iota(jnp.int32, (rows, rows), 0)
    jj = lax.broadcasted_iota(jnp.int32, (rows, rows), 1)
    diff = (ii - jj).astype(F32)
    keep = ((ii >> t_log2) == (jj >> t_log2)) & (ii >= jj)
    decay = jnp.where(keep, jnp.exp(jnp.maximum(diff, 0.0) * lg), 0.0)
    ri = lax.broadcasted_iota(jnp.int32, (rows, 1), 0)
    rb = ri >> t_log2
    ti = (ri & (ts - 1)).astype(F32)
    xi = jnp.exp((ti + 1.0) * lg)
    to_end = jnp.exp((ts - 1.0 - ti) * lg)
    g_chunk = jnp.exp(ts * lg)
    cos = cos_ref[...]
    sin = sin_ref[...]
    q = _rope128(q_ref[...], cos, sin)
    k = _rope128(k_ref[...], cos, sin) * k_scale
    qb = q.astype(BF16)
    vb = v_ref[...].astype(BF16)
    s = _dot_nt(qb, k.astype(BF16)) * decay
    o_in = _dot(s.astype(BF16), vb)
    kw = k * to_end
    o_cross = jnp.zeros(o_in.shape, F32)
    for b in range(nb):
        st = st_in_ref[b, 0]
        mine = rb == b
        o_cross = jnp.where(mine, _dot(qb, st.astype(BF16)), o_cross)
        kw_t = jnp.transpose(jnp.where(mine, kw, 0.0)).astype(BF16)
        st_out_ref[b, 0] = g_chunk * st + _dot(kw_t, vb)
    o_ref[...] = _group_norm_gate(o_in + o_cross * xi, g_ref[...], gn_ref[...])


def _ret_sample(z, cos128, sin128, lgam, gn_w, state, *, row0, dec_batch, dec_seq, n_heads, dk, dv, offs):
    rq_off, rk_off, rv_off, rg_off = offs
    t_log2 = dec_seq.bit_length() - 1
    assert 1 << t_log2 == dec_seq
    rows = LANES
    nb = rows // dec_seq
    nblk = dec_batch // nb
    rb0 = row0 // rows
    qk_spec = lambda off: pl.BlockSpec((rows, dk), lambda i, h: (rb0 + i, off // dk + h))
    v_spec = lambda off: pl.BlockSpec((rows, dv), lambda i, h: (rb0 + i, off // dv + h))
    tab = pl.BlockSpec((rows, LANES), lambda i, h: (rb0 + i, 0))
    return pl.pallas_call(
        functools.partial(_ret_sample_kernel, nb=nb, t_log2=t_log2, k_scale=dk ** -0.5),
        grid=(nblk, n_heads),
        in_specs=[
            qk_spec(rq_off), qk_spec(rk_off), v_spec(rv_off), v_spec(rg_off), tab, tab,
            pl.BlockSpec((1, 1, LANES), lambda i, h: (h, 0, 0)),
            pl.BlockSpec((1, dv), lambda i, h: (0, h)),
            pl.BlockSpec((nb, 1, dk, dv), lambda i, h: (i, h, 0, 0)),
        ],
        out_specs=[
            pl.BlockSpec((rows, dv), lambda i, h: (i, h)),
            pl.BlockSpec((nb, 1, dk, dv), lambda i, h: (i, h, 0, 0)),
        ],
        out_shape=[
            jax.ShapeDtypeStruct((dec_batch * dec_seq, n_heads * dv), BF16),
            jax.ShapeDtypeStruct(state.shape, F32),
        ],
        compiler_params=_params("parallel", "parallel"),
    )(z, z, z, z, cos128, sin128, lgam, gn_w, state)


def _softmax_rows(s):
    e = jnp.exp(s - jnp.max(s, axis=-1, keepdims=True))
    return e / jnp.sum(e, axis=-1, keepdims=True)


def _mem_attn_prompt_kernel(q_ref, k_ref, v_ref, o_ref, *, scale):
    s = _dot_nt(q_ref[...].astype(BF16), k_ref[...].astype(BF16)) * scale
    o_ref[...] = _dot(_softmax_rows(s).astype(BF16), v_ref[...].astype(BF16)).astype(BF16)


def _mem_attn_prompt(z, kv, *, batch, seq, mem_tokens, n_heads, hd, tq, mq_off, total_rows):
    nq = seq // tq
    return pl.pallas_call(
        functools.partial(_mem_attn_prompt_kernel, scale=hd ** -0.5),
        grid=(batch, n_heads, nq),
        in_specs=[
            pl.BlockSpec((tq, hd), lambda b, h, i: (b * nq + i, mq_off // hd + h)),
            pl.BlockSpec((mem_tokens, hd), lambda b, h, i: (b, h)),
            pl.BlockSpec((mem_tokens, hd), lambda b, h, i: (b, n_heads + h)),
        ],
        out_specs=pl.BlockSpec((tq, hd), lambda b, h, i: (b * nq + i, h)),
        out_shape=jax.ShapeDtypeStruct((total_rows, n_heads * hd), BF16),
        compiler_params=_params("parallel", "parallel", "parallel"),
    )(z, kv, kv)


def _mem_attn_sample_kernel(q_ref, k_ref, v_ref, o_ref, *, nb, n_heads, scale):
    rows = q_ref.shape[1]
    cols = k_ref.shape[1] * n_heads
    hd = k_ref.shape[3]
    row_h = lax.broadcasted_iota(jnp.int32, (rows, cols), 0) & (n_heads - 1)
    col_h = lax.broadcasted_iota(jnp.int32, (rows, cols), 1) & (n_heads - 1)
    own = row_h == col_h
    for b in range(nb):
        k = k_ref[b].reshape(cols, hd).astype(BF16)
        v = v_ref[b].reshape(cols, hd).astype(BF16)
        s = jnp.where(own, _dot_nt(q_ref[b].astype(BF16), k) * scale, NEG_BIG)
        o_ref[b] = _dot(_softmax_rows(s).astype(BF16), v).astype(BF16)


def _mem_attn_sample(mq, mem_k, mem_v, *, nb):
    dec_batch, rows, hd = mq.shape
    _, mem_tokens, n_heads, _ = mem_k.shape
    assert n_heads & (n_heads - 1) == 0
    cache = pl.BlockSpec((nb, mem_tokens, n_heads, hd), lambda i: (i, 0, 0, 0))
    return pl.pallas_call(
        functools.partial(_mem_attn_sample_kernel, nb=nb, n_heads=n_heads, scale=hd ** -0.5),
        grid=(dec_batch // nb,),
        in_specs=[pl.BlockSpec((nb, rows, hd), lambda i: (i, 0, 0)), cache, cache],
        out_specs=pl.BlockSpec((nb, rows, hd), lambda i: (i, 0, 0)),
        out_shape=jax.ShapeDtypeStruct((dec_batch, rows, hd), BF16),
        compiler_params=_params("parallel"),
    )(mq, mem_k, mem_v)


def _merge_kernel(oap_ref, obp_ref, ocp_ref, oas_ref, obs_ref, ocs_ref, wa_ref, wb_ref, wc_ref, ga_ref, gb_ref,
                  gc_ref, m_ref, *, n_first):
    def run(oa_ref, ob_ref, oc_ref):
        merged = jax.nn.sigmoid(ga_ref[...]) * _dot(oa_ref[...], wa_ref[...])
        merged = merged + jax.nn.sigmoid(gb_ref[...]) * _dot(ob_ref[...], wb_ref[...])
        merged = merged + jax.nn.sigmoid(gc_ref[...]) * _dot(oc_ref[...], wc_ref[...])
        m_ref[...] = merged.astype(BF16)

    i = pl.program_id(1)

    @pl.when(i < n_first)
    def _():
        run(oap_ref, obp_ref, ocp_ref)

    @pl.when(i >= n_first)
    def _():
        run(oas_ref, obs_ref, ocs_ref)


def _merge(prompt_outs, sample_outs, w_a, w_b, w_c, z, *, tm, tn, gz_off):
    tp = prompt_outs[0].shape[0]
    t = tp + sample_outs[0].shape[0]
    n_first = tp // tm
    d = w_a.shape[1]
    nj = d // tn
    act_p = lambda a: pl.BlockSpec((tm, a.shape[1]), lambda j, i: (jnp.minimum(i, n_first - 1), 0))
    act_s = lambda a: pl.BlockSpec((tm, a.shape[1]), lambda j, i: (jnp.maximum(i - n_first, 0), 0))
    wsp = lambda w: pl.BlockSpec((w.shape[0], tn), lambda j, i: (0, j))
    gate = lambda br: pl.BlockSpec((tm, tn), lambda j, i: (i, (gz_off + br * d) // tn + j))
    return pl.pallas_call(
        functools.partial(_merge_kernel, n_first=n_first),
        grid=(nj, t // tm),
        in_specs=[act_p(a) for a in prompt_outs] + [act_s(a) for a in sample_outs]
        + [wsp(w_a), wsp(w_b), wsp(w_c), gate(0), gate(1), gate(2)],
        out_specs=pl.BlockSpec((tm, tn), lambda j, i: (i, j)),
        out_shape=jax.ShapeDtypeStruct((t, d), BF16),
        compiler_params=_params("parallel", "parallel"),
    )(*prompt_outs, *sample_outs, w_a, w_b, w_c, z, z, z)


def _out_proj_kernel(x_ref, m_ref, wo_ref, nw_ref, wrh_ref, wrl_ref, br_ref, x1_ref, h_ref, lg_ref):
    x1 = x_ref[...] + _dot(m_ref[...], wo_ref[...])
    x1_ref[...] = x1
    h = _rms(x1, nw_ref[...])
    sub = h.shape[1] // LANES
    for c in range(sub):
        h_ref[pl.ds(c, h.shape[0], stride=sub), :] = h[:, c * LANES:(c + 1) * LANES]
    h_hi = h.astype(BF16)
    h_lo = (h - h_hi.astype(F32)).astype(BF16)
    w_hi = wrh_ref[...]
    lg_ref[...] = _dot(h_hi, w_hi) + _dot(h_lo, w_hi) + _dot(h_hi, wrl_ref[...]) + br_ref[...]


def _out_proj(x, merged, w_o, norm_w, w_router_hi, w_router_lo, b_router, *, tm):
    t, d = x.shape
    row = lambda i: (i, 0)
    const = lambda i: (0, 0)
    return pl.pallas_call(
        _out_proj_kernel,
        grid=(t // tm,),
        in_specs=[
            pl.BlockSpec((tm, d), row),
            pl.BlockSpec((tm, d), row),
            pl.BlockSpec((d, d), const),
            pl.BlockSpec((1, d), const),
            pl.BlockSpec((d, LANES), const),
            pl.BlockSpec((d, LANES), const),
            pl.BlockSpec((1, LANES), const),
        ],
        out_specs=[pl.BlockSpec((tm, d), row), pl.BlockSpec((tm * (d // LANES), LANES), row),
                   pl.BlockSpec((tm, LANES), row)],
        out_shape=[
            jax.ShapeDtypeStruct((t, d), F32),
            jax.ShapeDtypeStruct((t * (d // LANES), LANES), F32),
            jax.ShapeDtypeStruct((t, LANES), F32),
        ],
        compiler_params=_params("parallel"),
    )(x, merged, w_o, norm_w, w_router_hi, w_router_lo, b_router)


def _gather_rows(src_hbm, idx_ref, base, buf, sem, n_rows, sub=1):
    def issue(r, carry):
        src = src_hbm.at[pl.ds(idx_ref[base + r] * sub, sub), :]
        pltpu.make_async_copy(src, buf.at[pl.ds(r * sub, sub), :], sem).start()
        return carry

    lax.fori_loop(0, n_rows, issue, 0, unroll=8)


def _wait_rows(src_hbm, buf, sem, n_rows, sub=1):
    pltpu.make_async_copy(src_hbm.at[pl.ds(0, n_rows * sub), :], buf, sem).wait()


def _start_rows(src_hbm, idx_ref, base, buf, sem, n_rows, sub=1):
    for r in range(n_rows):
        src = src_hbm.at[pl.ds(idx_ref[base + r] * sub, sub), :]
        pltpu.make_async_copy(src, buf.at[pl.ds(r * sub, sub), :], sem).start()


def _experts_kernel(be_ref, nu_ref, tok_ref, h_hbm, wg_ref, wu_ref, wd_ref, y_ref, buf0, buf1, buf2, wg_b, wu_b,
                    wd_b, sems, *, blk):
    i = pl.program_id(0)
    n_used = nu_ref[0]
    bufs = (buf0, buf1, buf2)
    sub = buf0.shape[0] // blk

    @pl.when(i == 0)
    def _():
        _gather_rows(h_hbm, tok_ref, 0, buf0, sems.at[0], blk, sub)
        _gather_rows(h_hbm, tok_ref, jnp.minimum(1, n_used - 1) * blk, buf1, sems.at[1], blk, sub)

    @pl.when((i < n_used) & ((i == 0) | (be_ref[i] != be_ref[jnp.maximum(i - 1, 0)])))
    def _():
        wg_b[...] = wg_ref[...].astype(BF16)
        wu_b[...] = wu_ref[...].astype(BF16)
        wd_b[...] = wd_ref[...].astype(BF16)

    def run(k):
        cur, mid, nxt = bufs[k], bufs[(k + 1) % 3], bufs[(k + 2) % 3]
        _wait_rows(h_hbm, cur, sems.at[k], blk, sub)
        _start_rows(h_hbm, tok_ref, jnp.minimum(i + 2, n_used - 1) * blk, nxt, sems.at[(k + 2) % 3], blk, sub)
        xb = jnp.concatenate([cur[pl.ds(c, blk, stride=sub), :] for c in range(sub)], axis=1).astype(BF16)
        g = _dot(xb, wg_b[...])
        u = _dot(xb, wu_b[...])
        a = (g * jax.nn.sigmoid(g) * u).astype(BF16)
        y_ref[...] = _dot(a, wd_b[...])

        @pl.when(i == n_used - 1)
        def _():
            _wait_rows(h_hbm, mid, sems.at[(k + 1) % 3], blk, sub)
            _wait_rows(h_hbm, nxt, sems.at[(k + 2) % 3], blk, sub)

    for k in range(3):
        @pl.when((i < n_used) & (i % 3 == k))
        def _():
            run(k)

    @pl.when(i >= n_used)
    def _():
        y_ref[...] = jnp.zeros(y_ref.shape, F32)


def _experts(blk_expert, n_used, row_tok, h, w_gate, w_up, w_down, *, n_blk, blk):
    d = w_gate.shape[1]
    de = w_gate.shape[2]
    grid_spec = pltpu.PrefetchScalarGridSpec(
        num_scalar_prefetch=3,
        grid=(n_blk,),
        in_specs=[
            pl.BlockSpec(memory_space=pl.ANY),
            pl.BlockSpec((None, d, de), lambda i, be, nu, tok: (be[i], 0, 0)),
            pl.BlockSpec((None, d, de), lambda i, be, nu, tok: (be[i], 0, 0)),
            pl.BlockSpec((None, de, d), lambda i, be, nu, tok: (be[i], 0, 0)),
        ],
        out_specs=pl.BlockSpec((blk, d), lambda i, be, nu, tok: (i, 0)),
        scratch_shapes=[
            pltpu.VMEM((blk * (d // LANES), LANES), F32),
            pltpu.VMEM((blk * (d // LANES), LANES), F32),
            pltpu.VMEM((blk * (d // LANES), LANES), F32),
            pltpu.VMEM((d, de), BF16),
            pltpu.VMEM((d, de), BF16),
            pltpu.VMEM((de, d), BF16),
            pltpu.SemaphoreType.DMA((3,)),
        ],
    )
    return pl.pallas_call(
        functools.partial(_experts_kernel, blk=blk),
        grid_spec=grid_spec,
        out_shape=jax.ShapeDtypeStruct((n_blk * blk, d), F32),
        compiler_params=_params("arbitrary"),
    )(blk_expert, n_used, row_tok, h, w_gate, w_up, w_down)


def _combine_kernel(d0_ref, d1_ref, y_hbm, x1_ref, g0_ref, g1_ref, nw_ref, op_ref, os_ref, b00, b01, b02, b10, b11,
                    b12, sems, *, tm, n_first):
    i = pl.program_id(0)
    last = pl.num_programs(0) - 1
    bufs0 = (b00, b01, b02)
    bufs1 = (b10, b11, b12)

    def start(step, k, straight):
        issue = _start_rows if straight else _gather_rows
        issue(y_hbm, d0_ref, step * tm, bufs0[k], sems.at[0, k], tm)
        issue(y_hbm, d1_ref, step * tm, bufs1[k], sems.at[1, k], tm)

    def wait(k):
        _wait_rows(y_hbm, bufs0[k], sems.at[0, k], tm)
        _wait_rows(y_hbm, bufs1[k], sems.at[1, k], tm)

    @pl.when(i == 0)
    def _():
        start(0, 0, False)
        start(jnp.minimum(1, last), 1, False)

    def run(k):
        wait(k)
        start(jnp.minimum(i + 2, last), (k + 2) % 3, True)
        reps = x1_ref.shape[1] // LANES
        g0 = jnp.tile(g0_ref[...], (1, reps))
        g1 = jnp.tile(g1_ref[...], (1, reps))
        x2 = x1_ref[...] + (bufs0[k][...] * g0 + bufs1[k][...] * g1)
        y = _rms(x2, nw_ref[...])

        @pl.when(i < n_first)
        def _():
            op_ref[...] = y

        @pl.when(i >= n_first)
        def _():
            os_ref[...] = y

        @pl.when(i == last)
        def _():
            wait((k + 1) % 3)
            wait((k + 2) % 3)

    for k in range(3):
        @pl.when(i % 3 == k)
        def _():
            run(k)


def _combine(d0, d1, y_rows, x1, g0, g1, norm_w, *, tm, rows_first):
    t, d = x1.shape
    n_first = rows_first // tm
    row = lambda i, a, b: (i, 0)
    grid_spec = pltpu.PrefetchScalarGridSpec(
        num_scalar_prefetch=2,
        grid=(t // tm,),
        in_specs=[
            pl.BlockSpec(memory_space=pl.ANY),
            pl.BlockSpec((tm, d), row),
            pl.BlockSpec((tm, LANES), row),
            pl.BlockSpec((tm, LANES), row),
            pl.BlockSpec((1, d), lambda i, a, b: (0, 0)),
        ],
        out_specs=[
            pl.BlockSpec((tm, d), lambda i, a, b: (jnp.minimum(i, n_first - 1), 0)),
            pl.BlockSpec((tm, d), lambda i, a, b: (jnp.maximum(i - n_first, 0), 0)),
        ],
        scratch_shapes=[pltpu.VMEM((tm, d), F32)] * 6 + [pltpu.SemaphoreType.DMA((2, 3))],
    )
    return pl.pallas_call(
        functools.partial(_combine_kernel, tm=tm, n_first=n_first),
        grid_spec=grid_spec,
        out_shape=[jax.ShapeDtypeStruct((rows_first, d), F32), jax.ShapeDtypeStruct((t - rows_first, d), F32)],
        compiler_params=_params("arbitrary"),
    )(d0, d1, y_rows, x1, g0, g1, norm_w)


def _route(logits, n_groups, per_group, top_k, blk):
    t = logits.shape[0]
    n_exp = n_groups * per_group
    assert top_k == 2
    grp_p = jax.nn.softmax(logits[:, :n_groups], axis=-1)
    g_idx = jnp.argmax(grp_p, axis=-1, keepdims=True)
    g_prob = jnp.max(grp_p, axis=-1, keepdims=True)
    e_logit = logits[:, n_groups:n_groups + n_exp].reshape(t, n_groups, per_group)
    in_grp = jnp.take_along_axis(e_logit, g_idx[:, :, None], axis=1)[:, 0, :]
    i1 = jnp.argmax(in_grp, axis=-1, keepdims=True)
    rest = jnp.where(jnp.arange(per_group)[None, :] == i1, -jnp.inf, in_grp)
    i2 = jnp.argmax(rest, axis=-1, keepdims=True)
    top_i = jnp.concatenate([i1, i2], axis=-1)
    top_v = jnp.concatenate([jnp.max(in_grp, axis=-1, keepdims=True), jnp.max(rest, axis=-1, keepdims=True)], axis=-1)
    gate = g_prob * jax.nn.softmax(top_v, axis=-1)
    expert = (g_idx * per_group + top_i).reshape(-1).astype(jnp.int32)
    a = t * top_k
    onehot = expert[:, None] == jnp.arange(n_exp, dtype=jnp.int32)[None, :]
    csum = jnp.cumsum(onehot.astype(jnp.int32), axis=0)
    counts = csum[-1]
    rank = jnp.sum(jnp.where(onehot, csum - 1, 0), axis=1)
    padded = (counts + blk - 1) // blk * blk
    pad_end = jnp.cumsum(padded)
    pad_start = pad_end - padded
    dest = (pad_start[expert] + rank).astype(jnp.int32)
    n_blk = (a + n_exp * (blk - 1) + blk - 1) // blk
    tok = jnp.arange(a, dtype=jnp.int32) // top_k
    row_tok = jnp.zeros((n_blk * blk,), jnp.int32).at[dest].set(tok)
    blk_start = jnp.arange(n_blk, dtype=jnp.int32) * blk
    blk_expert = jnp.minimum(jnp.sum(pad_end[None, :] <= blk_start[:, None], axis=1), n_exp - 1).astype(jnp.int32)
    n_used = (pad_end[-1:] // blk).astype(jnp.int32)
    return gate, dest.reshape(t, top_k), row_tok, blk_expert, n_used, n_blk


def _pick(n, *prefs):
    for p in prefs:
        if n % p == 0:
            return p
    return n


def kernel(x_prompt, x_sample, mem_prompt, cache_ckv, cache_krope, state_ret, cache_mem_k, cache_mem_v, page_table, attn_norm_w, w_in, mla_q_norm_w, mla_w_uq, mla_kv_norm_w, mla_w_uk, mla_w_uv, ret_gn_w, mem_norm_w, mem_w_kv, w_branch_a, w_branch_b, w_branch_c, w_out, ffn_norm_w, router_grp_w, router_grp_b, router_exp_w, router_exp_b, exp_w_gate, exp_w_up, exp_w_down, final_norm_w):
    batch, seq, d = x_prompt.shape
    dec_batch, dec_seq, _ = x_sample.shape
    depth = w_in.shape[0]
    q_rank, n_heads, qh = mla_w_uq.shape[1:]
    kv_rank, _, nope = mla_w_uk.shape[1:]
    rope_dim = qh - nope
    v_dim = mla_w_uv.shape[3]
    ret_heads, ret_dv = ret_gn_w.shape[1:]
    ret_dk = state_ret.shape[3]
    mem_tokens = mem_prompt.shape[1]
    mem_heads, mem_hd = cache_mem_k.shape[3:]
    mem_width = mem_heads * mem_hd
    n_groups = router_grp_w.shape[2]
    n_exp = router_exp_w.shape[2]
    per_group = n_exp // n_groups
    top_k = 2
    page = cache_ckv.shape[2]
    past_len = page_table.shape[1] * page
    assert nope == LANES and v_dim == LANES and rope_dim == 64 and ret_dk == LANES
    assert q_rank == kv_rank and n_groups + n_exp <= LANES

    tp = batch * seq
    ts = dec_batch * dec_seq
    t = tp + ts
    ret_qk = ret_heads * ret_dk
    ret_v = ret_heads * ret_dv
    cq_off, ckv_off = 0, q_rank
    rq_off = ckv_off + kv_rank
    rk_off = rq_off + ret_qk
    rv_off = rk_off + ret_qk
    rg_off = rv_off + ret_v
    mq_off = rg_off + ret_v
    gz_off = mq_off + mem_width
    ret_offs = (rq_off, rk_off, rv_off, rg_off)

    pos = jnp.concatenate([jnp.tile(jnp.arange(seq), batch), jnp.tile(past_len + jnp.arange(dec_seq), dec_batch)])
    cos64, sin64 = _rope_tables(pos, rope_dim)
    cos128, sin128 = _rope_tables(pos, ret_dk)
    lgam = jnp.log1p(-jnp.exp2(-5.0 - jnp.arange(ret_heads, dtype=F32)))
    lgam = jnp.broadcast_to(lgam[:, None, None], (ret_heads, 1, LANES))

    xp = x_prompt.reshape(tp, d)
    xs = x_sample.reshape(ts, d)
    x = jnp.concatenate([xp, xs], axis=0)

    assert depth == 1, "the fused final norm assumes a single layer"
    l = 0
    tm_mid = _pick(t, 256)

    wi = w_in[l]
    kr_lo = q_rank + kv_rank
    w_main = jnp.concatenate([wi[:, :kr_lo], wi[:, kr_lo + rope_dim:]], axis=1).astype(BF16)
    w_kr = jnp.pad(wi[:, kr_lo:kr_lo + rope_dim], ((0, 0), (0, LANES - rope_dim))).astype(BF16)
    z, zk = _norm_proj(x, attn_norm_w[l], w_main, w_kr, tm=_pick(t, 1536, 1024, 512),
                       tn=_pick(w_main.shape[1], 512))

    wuq = mla_w_uq[l]
    wqn = wuq[:, :, :nope].reshape(q_rank, n_heads * LANES).astype(BF16)
    wqr = jnp.pad(wuq[:, :, nope:], ((0, 0), (0, 0), (0, LANES - rope_dim))).reshape(q_rank, n_heads * LANES).astype(BF16)
    wuk = mla_w_uk[l].reshape(kv_rank, n_heads * nope).astype(BF16)
    wuv = mla_w_uv[l].reshape(kv_rank, n_heads * v_dim).astype(BF16)
    qcat, c, kr, kcat, v = _mla_prep(
        z, zk, cos64, sin64, mla_q_norm_w[l].reshape(1, -1), mla_kv_norm_w[l].reshape(1, -1), wqn, wqr, wuk, wuv,
        tm=tm_mid, n_heads=n_heads, q_rank=q_rank, kv_rank=kv_rank, rope_dim=rope_dim)

    scale = qh ** -0.5
    oa_p = _mla_prompt(qcat, kcat, v, batch=batch, seq=seq, n_heads=n_heads, tq=_pick(seq, 512), scale=scale,
                       total_rows=tp)
    wukt = jnp.transpose(mla_w_uk[l], (1, 2, 0)).astype(BF16)
    qlat, qr = _absorb_q(qcat, wukt, ts=ts, row_block=tp // ts, n_heads=n_heads, kv_rank=kv_rank)
    rows = dec_seq * n_heads
    new_pad = LANES - dec_seq
    cnew = jnp.pad(c[tp:].reshape(dec_batch, dec_seq, kv_rank), ((0, 0), (0, new_pad), (0, 0))).astype(BF16)
    krnew = jnp.pad(kr[tp:].reshape(dec_batch, dec_seq, rope_dim), ((0, 0), (0, new_pad), (0, 0))).astype(BF16)
    o_lat = _mla_decode(
        page_table, qlat.reshape(dec_batch, rows, kv_rank), qr.reshape(dec_batch, rows, LANES), cnew, krnew,
        cache_ckv[l], jnp.swapaxes(cache_krope[l], 1, 2), npg=_pick(page_table.shape[1], 32), scale=scale,
        n_heads=n_heads, rope_dim=rope_dim)
    wuv_h = jnp.transpose(mla_w_uv[l], (1, 0, 2)).astype(BF16)
    oa_s = _head_proj(o_lat.reshape(ts, n_heads * kv_rank), wuv_h, n_heads=n_heads)

    gn_w = ret_gn_w[l].reshape(1, ret_v)
    ob_p, st_p = _ret_prompt(z, cos128, sin128, lgam, gn_w, batch=batch, seq=seq, n_heads=ret_heads, dk=ret_dk,
                             dv=ret_dv, rows=_pick(seq, 1024, 512), offs=ret_offs, total_rows=tp)
    ob_s, st_s = _ret_sample(z, cos128, sin128, lgam, gn_w, state_ret[l], row0=tp, dec_batch=dec_batch,
                             dec_seq=dec_seq, n_heads=ret_heads, dk=ret_dk, dv=ret_dv, offs=ret_offs)

    mem_rows = batch * mem_tokens
    (kv,) = _norm_proj(mem_prompt.reshape(mem_rows, d), mem_norm_w[l], mem_w_kv[l].astype(BF16),
                       tm=_pick(mem_rows, 1024, 512), tn=_pick(2 * mem_width, 512))
    oc_p = _mem_attn_prompt(z, kv, batch=batch, seq=seq, mem_tokens=mem_tokens, n_heads=mem_heads, hd=mem_hd,
                            tq=_pick(seq, 512), mq_off=mq_off, total_rows=tp)
    mq_s = z[tp:, mq_off:mq_off + mem_width].reshape(dec_batch, dec_seq * mem_heads, mem_hd)
    oc_s = _mem_attn_sample(mq_s, cache_mem_k[l], cache_mem_v[l], nb=4).reshape(ts, mem_width)

    tm_merge = _pick(ts, 512, 256, 128)
    assert tp % tm_merge == 0
    merged = _merge((oa_p, ob_p, oc_p), (oa_s, ob_s, oc_s), w_branch_a[l].astype(BF16), w_branch_b[l].astype(BF16),
                    w_branch_c[l].astype(BF16), z, tm=tm_merge, tn=_pick(d, 1024), gz_off=gz_off)
    n_route = n_groups + n_exp
    w_router = jnp.pad(jnp.concatenate([router_grp_w[l], router_exp_w[l]], axis=1), ((0, 0), (0, LANES - n_route)))
    w_router_hi = w_router.astype(BF16)
    w_router_lo = (w_router - w_router_hi.astype(F32)).astype(BF16)
    b_router = jnp.pad(jnp.concatenate([router_grp_b[l], router_exp_b[l]]), (0, LANES - n_route)).reshape(1, LANES)
    x1, h2, logits = _out_proj(x, merged, w_out[l].astype(BF16), ffn_norm_w[l].reshape(1, d), w_router_hi,
                               w_router_lo, b_router, tm=tm_mid)

    gate, dest, row_tok, blk_expert, n_used, n_blk = _route(logits, n_groups, per_group, top_k, MOE_BLOCK)
    y_rows = _experts(blk_expert, n_used, row_tok, h2, exp_w_gate[l], exp_w_up[l], exp_w_down[l], n_blk=n_blk,
                      blk=MOE_BLOCK)
    g0 = jnp.broadcast_to(gate[:, 0:1], (t, LANES))
    g1 = jnp.broadcast_to(gate[:, 1:2], (t, LANES))
    tm_c = _pick(ts, 128)
    assert tp % tm_c == 0
    y_p, y_s = _combine(dest[:, 0], dest[:, 1], y_rows, x1, g0, g1, final_norm_w.reshape(1, d), tm=tm_c, rows_first=tp)

    lead = lambda a: a[None]
    return (
        y_p.reshape(batch, seq, d),
        y_s.reshape(dec_batch, dec_seq, d),
        lead(c[:tp].reshape(batch, seq, kv_rank)),
        lead(kr[:tp].reshape(batch, seq, rope_dim)),
        lead(st_p),
        lead(kv[:, :mem_width].reshape(batch, mem_tokens, mem_heads, mem_hd)),
        lead(kv[:, mem_width:].reshape(batch, mem_tokens, mem_heads, mem_hd)),
        lead(c[tp:].reshape(dec_batch, dec_seq, kv_rank)),
        lead(kr[tp:].reshape(dec_batch, dec_seq, rope_dim)),
        lead(st_s),
    )
```

```python
import functools

import jax
import jax.numpy as jnp
from jax import lax
from jax.experimental import pallas as pl
from jax.experimental.pallas import tpu as pltpu

F32 = jnp.float32
BF16 = jnp.bfloat16

NORM_EPS = 1e-6
GN_EPS = 1e-5
ROPE_THETA = 10000.0
RET_CHUNK = 128
MOE_BLOCK = 256
DECODE_SLOTS = 3
LANES = 128
NEG_BIG = -1e30
LOG2_E = 1.4426950408889634
VMEM_LIMIT_BYTES = 56 * 1024 * 1024


def _params(*sem):
    return pltpu.CompilerParams(dimension_semantics=sem, vmem_limit_bytes=VMEM_LIMIT_BYTES)


def _dot(a, b):
    return jnp.dot(a, b, preferred_element_type=F32)


def _dot_nt(a, b):
    return lax.dot_general(a, b, (((1,), (1,)), ((), ())), preferred_element_type=F32)


def _rms(x, w):
    return x * lax.rsqrt(jnp.mean(x * x, axis=-1, keepdims=True) + NORM_EPS) * w


def _rope64(x, cos, sin):
    lane = lax.broadcasted_iota(jnp.int32, x.shape, 1)
    first = (lane & 63) < 32
    swapped = jnp.where(first, pltpu.roll(x, 96, 1), pltpu.roll(x, 32, 1))
    return x * cos + swapped * sin


def _rope128(x, cos, sin):
    return x * cos + pltpu.roll(x, 64, 1) * sin


def _rope_tables(pos, dim):
    half = dim // 2
    inv = ROPE_THETA ** (-jnp.arange(half, dtype=F32) / half)
    ang = pos.astype(F32)[:, None] * inv[None, :]
    cos = jnp.cos(ang)
    sin = jnp.sin(ang)
    c = jnp.concatenate([cos, cos], axis=-1)
    s = jnp.concatenate([-sin, sin], axis=-1)
    reps = LANES // dim
    return jnp.tile(c, (1, reps)), jnp.tile(s, (1, reps))


def _norm_proj_kernel(x_ref, nw_ref, w_ref, z_ref, h_ref):
    @pl.when(pl.program_id(1) == 0)
    def _():
        h_ref[...] = _rms(x_ref[...], nw_ref[...]).astype(BF16)

    z_ref[...] = _dot(h_ref[...], w_ref[...])


def _norm_proj_extra_kernel(x_ref, nw_ref, w_ref, we_ref, z_ref, ze_ref, h_ref):
    @pl.when(pl.program_id(1) == 0)
    def _():
        h = _rms(x_ref[...], nw_ref[...]).astype(BF16)
        h_ref[...] = h
        ze_ref[...] = _dot(h, we_ref[...])

    z_ref[...] = _dot(h_ref[...], w_ref[...])


def _norm_proj(x, norm_w, w, w_extra=None, *, tm, tn):
    t, d = x.shape
    n = w.shape[1]
    grid = (t // tm, n // tn)
    in_specs = [
        pl.BlockSpec((tm, d), lambda i, j: (i, 0)),
        pl.BlockSpec((1, d), lambda i, j: (0, 0)),
        pl.BlockSpec((d, tn), lambda i, j: (0, j)),
    ]
    out_specs = [pl.BlockSpec((tm, tn), lambda i, j: (i, j))]
    out_shape = [jax.ShapeDtypeStruct((t, n), F32)]
    args = [x, norm_w.reshape(1, d), w]
    body = _norm_proj_kernel
    if w_extra is not None:
        ne = w_extra.shape[1]
        in_specs.append(pl.BlockSpec((d, ne), lambda i, j: (0, 0)))
        out_specs.append(pl.BlockSpec((tm, ne), lambda i, j: (i, 0)))
        out_shape.append(jax.ShapeDtypeStruct((t, ne), F32))
        args.append(w_extra)
        body = _norm_proj_extra_kernel
    return pl.pallas_call(
        body,
        grid=grid,
        in_specs=in_specs,
        out_specs=out_specs,
        out_shape=out_shape,
        scratch_shapes=[pltpu.VMEM((tm, d), BF16)],
        compiler_params=_params("parallel", "arbitrary"),
    )(*args)


def _mla_prep_kernel(cq_ref, ckv_ref, zk_ref, cos_ref, sin_ref, qnw_ref, kvnw_ref, wqn_ref, wqr_ref,
                     wuk_ref, wuv_ref, qcat_ref, c_ref, kr_ref, kcat_ref, v_ref, *, n_heads, rope_dim):
    cos = cos_ref[...]
    sin = sin_ref[...]
    cqn = _rms(cq_ref[...], qnw_ref[...]).astype(BF16)
    qn = _dot(cqn, wqn_ref[...])
    qr = _dot(cqn, wqr_ref[...])
    c = _rms(ckv_ref[...], kvnw_ref[...])
    c_ref[...] = c
    cb = c.astype(BF16)
    kn = _dot(cb, wuk_ref[...])
    v_ref[...] = _dot(cb, wuv_ref[...]).astype(BF16)
    kr = _rope64(zk_ref[...], cos, sin)
    kr_ref[...] = kr[:, :rope_dim]
    krb = kr.astype(BF16)
    for h in range(n_heads):
        lo = h * 2 * LANES
        sl = slice(h * LANES, (h + 1) * LANES)
        qcat_ref[:, lo:lo + LANES] = qn[:, sl].astype(BF16)
        qcat_ref[:, lo + LANES:lo + 2 * LANES] = _rope64(qr[:, sl], cos, sin).astype(BF16)
        kcat_ref[:, lo:lo + LANES] = kn[:, sl].astype(BF16)
        kcat_ref[:, lo + LANES:lo + 2 * LANES] = krb


def _mla_prep(z, zk, cos64, sin64, qnw, kvnw, wqn, wqr, wuk, wuv, *, tm, n_heads, q_rank, kv_rank, rope_dim):
    t = z.shape[0]
    row = lambda i: (i, 0)
    const = lambda i: (0, 0)
    hw = n_heads * LANES
    return pl.pallas_call(
        functools.partial(_mla_prep_kernel, n_heads=n_heads, rope_dim=rope_dim),
        grid=(t // tm,),
        in_specs=[
            pl.BlockSpec((tm, q_rank), lambda i: (i, 0)),
            pl.BlockSpec((tm, kv_rank), lambda i: (i, q_rank // kv_rank)),
            pl.BlockSpec((tm, LANES), row),
            pl.BlockSpec((tm, LANES), row),
            pl.BlockSpec((tm, LANES), row),
            pl.BlockSpec((1, q_rank), const),
            pl.BlockSpec((1, kv_rank), const),
            pl.BlockSpec((q_rank, hw), const),
            pl.BlockSpec((q_rank, hw), const),
            pl.BlockSpec((kv_rank, hw), const),
            pl.BlockSpec((kv_rank, hw), const),
        ],
        out_specs=[
            pl.BlockSpec((tm, 2 * hw), row),
            pl.BlockSpec((tm, kv_rank), row),
            pl.BlockSpec((tm, rope_dim), row),
            pl.BlockSpec((tm, 2 * hw), row),
            pl.BlockSpec((tm, hw), row),
        ],
        out_shape=[
            jax.ShapeDtypeStruct((t, 2 * hw), BF16),
            jax.ShapeDtypeStruct((t, kv_rank), F32),
            jax.ShapeDtypeStruct((t, rope_dim), F32),
            jax.ShapeDtypeStruct((t, 2 * hw), BF16),
            jax.ShapeDtypeStruct((t, hw), BF16),
        ],
        compiler_params=_params("parallel"),
    )(z, z, zk, cos64, sin64, qnw, kvnw, wqn, wqr, wuk, wuv)


def _flash_kernel(q_ref, k_ref, v_ref, o_ref, *, tq, scale):
    qi = pl.program_id(2)
    q = q_ref[...]

    def block(j, carry, diagonal):
        m, l, acc = carry
        start = pl.multiple_of(j * tq, tq)
        k = k_ref[pl.ds(start, tq), :]
        v = v_ref[pl.ds(start, tq), :]
        s = _dot_nt(q, k) * (scale * LOG2_E)
        if diagonal:
            rows = lax.broadcasted_iota(jnp.int32, (tq, tq), 0)
            cols = lax.broadcasted_iota(jnp.int32, (tq, tq), 1)
            s = jnp.where(cols <= rows, s, NEG_BIG)
        m_new = jnp.maximum(m, jnp.max(s, axis=-1, keepdims=True))
        alpha = jnp.exp2(m - m_new)
        p = jnp.exp2(s - m_new)
        l = alpha * l + jnp.sum(p, axis=-1, keepdims=True)
        acc = alpha * acc + _dot(p.astype(BF16), v)
        return m_new, l, acc

    dv = v_ref.shape[-1]
    init = (jnp.full((tq, 1), NEG_BIG, F32), jnp.zeros((tq, 1), F32), jnp.zeros((tq, dv), F32))
    carry = lax.fori_loop(0, qi, functools.partial(block, diagonal=False), init)
    _, l, acc = block(qi, carry, True)
    o_ref[...] = (acc / l).astype(BF16)


def _mla_prompt(qcat, kcat, v, *, batch, seq, n_heads, tq, scale, total_rows):
    nq = seq // tq
    dv = v.shape[1] // n_heads
    return pl.pallas_call(
        functools.partial(_flash_kernel, tq=tq, scale=scale),
        grid=(batch, n_heads, nq),
        in_specs=[
            pl.BlockSpec((tq, 2 * LANES), lambda b, h, i: (b * nq + i, h)),
            pl.BlockSpec((seq, 2 * LANES), lambda b, h, i: (b, h)),
            pl.BlockSpec((seq, dv), lambda b, h, i: (b, h)),
        ],
        out_specs=pl.BlockSpec((tq, dv), lambda b, h, i: (b * nq + i, h)),
        out_shape=jax.ShapeDtypeStruct((total_rows, n_heads * dv), BF16),
        compiler_params=_params("parallel", "parallel", "arbitrary"),
    )(qcat, kcat, v)


def _absorb_q_kernel(qcat_ref, wukt_ref, qlat_ref, qr_ref, *, n_heads, kv_rank):
    for h in range(n_heads):
        lo = h * 2 * LANES
        qlat_ref[:, h * kv_rank:(h + 1) * kv_rank] = _dot(qcat_ref[:, lo:lo + LANES], wukt_ref[h]).astype(BF16)
        qr_ref[:, h * LANES:(h + 1) * LANES] = qcat_ref[:, lo + LANES:lo + 2 * LANES]


def _absorb_q(qcat, wukt, *, ts, row_block, n_heads, kv_rank):
    return pl.pallas_call(
        functools.partial(_absorb_q_kernel, n_heads=n_heads, kv_rank=kv_rank),
        grid=(1,),
        in_specs=[
            pl.BlockSpec((ts, qcat.shape[1]), lambda i: (row_block, 0)),
            pl.BlockSpec(wukt.shape, lambda i: (0, 0, 0)),
        ],
        out_specs=[
            pl.BlockSpec((ts, n_heads * kv_rank), lambda i: (0, 0)),
            pl.BlockSpec((ts, n_heads * LANES), lambda i: (0, 0)),
        ],
        out_shape=[
            jax.ShapeDtypeStruct((ts, n_heads * kv_rank), BF16),
            jax.ShapeDtypeStruct((ts, n_heads * LANES), BF16),
        ],
        compiler_params=_params("arbitrary"),
    )(qcat, wukt)


def _decode_kernel(pt_ref, qlat_ref, qr_ref, cnew_ref, krnew_ref, ckv_hbm, krt_hbm, o_ref, cbuf, kbuf, cb_ref, kb_ref,
                   m_ref, l_ref, acc_ref, sems, *, npg, scale, n_heads, rope_dim):
    b = pl.program_id(0)
    step = pl.program_id(1)
    steps = pl.num_programs(1)
    total = pl.num_programs(0) * steps
    n_slots = cbuf.shape[0]
    g_step = b * steps + step
    slot = g_step % n_slots
    page = cbuf.shape[2]

    def fetch(g):
        bb = g // steps
        ss = g % steps
        sl = g % n_slots

        def body(j, carry):
            pg = pt_ref[bb, ss * npg + j]
            pltpu.make_async_copy(ckv_hbm.at[pg], cbuf.at[sl, j], sems.at[0, sl]).start()
            pltpu.make_async_copy(krt_hbm.at[pg], kbuf.at[sl, j], sems.at[1, sl]).start()
            return carry

        lax.fori_loop(0, npg, body, 0, unroll=8)

    @pl.when(g_step == 0)
    def _():
        for g0 in range(n_slots - 1):
            @pl.when(g0 < total)
            def _():
                fetch(jnp.int32(g0))

    @pl.when(g_step + n_slots - 1 < total)
    def _():
        fetch(g_step + n_slots - 1)

    pltpu.make_async_copy(ckv_hbm.at[pl.ds(0, npg)], cbuf.at[slot], sems.at[0, slot]).wait()
    pltpu.make_async_copy(krt_hbm.at[pl.ds(0, npg)], kbuf.at[slot], sems.at[1, slot]).wait()

    @pl.when(step == 0)
    def _():
        m_ref[...] = jnp.full(m_ref.shape, NEG_BIG, F32)
        l_ref[...] = jnp.zeros(l_ref.shape, F32)
        acc_ref[...] = jnp.zeros(acc_ref.shape, F32)

    ql = qlat_ref[0]
    qr = qr_ref[0][:, :rope_dim]

    def update(s, vals):
        m_prev = m_ref[...]
        m_new = jnp.maximum(m_prev, jnp.max(s, axis=-1, keepdims=True))
        alpha = jnp.exp(m_prev - m_new)
        p = jnp.exp(s - m_new)
        l_ref[...] = alpha * l_ref[...] + jnp.sum(p, axis=-1, keepdims=True)
        acc_ref[...] = alpha * acc_ref[...] + _dot(p.astype(BF16), vals)
        m_ref[...] = m_new

    for j in range(npg):
        cb_ref[j * page:(j + 1) * page, :] = cbuf[slot, j].astype(BF16)
        kb_ref[:, j * page:(j + 1) * page] = kbuf[slot, j].astype(BF16)
    cb = cb_ref[...]
    update((_dot_nt(ql, cb) + _dot(qr, kb_ref[...])) * scale, cb)

    @pl.when(step == pl.num_programs(1) - 1)
    def _():
        cn = cnew_ref[0]
        krn = krnew_ref[0]
        s = (_dot_nt(ql, cn) + _dot_nt(qr, krn)) * scale
        t_row = lax.broadcasted_iota(jnp.int32, s.shape, 0) >> (n_heads.bit_length() - 1)
        col = lax.broadcasted_iota(jnp.int32, s.shape, 1)
        update(jnp.where(col <= t_row, s, NEG_BIG), cn)
        o_ref[0] = (acc_ref[...] / l_ref[...]).astype(BF16)


def _mla_decode(page_table, qlat, qr, cnew, krnew, cache_ckv, cache_krope_t, *, npg, scale, n_heads, rope_dim):
    nb, rows, kv_rank = qlat.shape
    n_pages = page_table.shape[1]
    page = cache_ckv.shape[1]
    steps = n_pages // npg
    per_b = lambda b, p, pt: (b, 0, 0)
    grid_spec = pltpu.PrefetchScalarGridSpec(
        num_scalar_prefetch=1,
        grid=(nb, steps),
        in_specs=[
            pl.BlockSpec((1, rows, kv_rank), per_b),
            pl.BlockSpec((1, rows, LANES), per_b),
            pl.BlockSpec((1,) + cnew.shape[1:], per_b),
            pl.BlockSpec((1,) + krnew.shape[1:], per_b),
            pl.BlockSpec(memory_space=pl.ANY),
            pl.BlockSpec(memory_space=pl.ANY),
        ],
        out_specs=pl.BlockSpec((1, rows, kv_rank), per_b),
        scratch_shapes=[
            pltpu.VMEM((DECODE_SLOTS, npg, page, kv_rank), F32),
            pltpu.VMEM((DECODE_SLOTS, npg, rope_dim, page), F32),
            pltpu.VMEM((npg * page, kv_rank), BF16),
            pltpu.VMEM((rope_dim, npg * page), BF16),
            pltpu.VMEM((rows, 1), F32),
            pltpu.VMEM((rows, 1), F32),
            pltpu.VMEM((rows, kv_rank), F32),
            pltpu.SemaphoreType.DMA((2, DECODE_SLOTS)),
        ],
    )
    return pl.pallas_call(
        functools.partial(_decode_kernel, npg=npg, scale=scale, n_heads=n_heads, rope_dim=rope_dim),
        grid_spec=grid_spec,
        out_shape=jax.ShapeDtypeStruct((nb, rows, kv_rank), BF16),
        compiler_params=_params("arbitrary", "arbitrary"),
    )(page_table, qlat, qr, cnew, krnew, cache_ckv, cache_krope_t)


def _head_proj_kernel(x_ref, w_ref, o_ref, *, n_heads):
    k = x_ref.shape[1] // n_heads
    n = o_ref.shape[1] // n_heads
    for h in range(n_heads):
        o_ref[:, h * n:(h + 1) * n] = _dot(x_ref[:, h * k:(h + 1) * k], w_ref[h]).astype(BF16)


def _head_proj(x, w, *, n_heads):
    rows = x.shape[0]
    n = w.shape[2]
    return pl.pallas_call(
        functools.partial(_head_proj_kernel, n_heads=n_heads),
        grid=(1,),
        in_specs=[pl.BlockSpec(x.shape, lambda i: (0, 0)), pl.BlockSpec(w.shape, lambda i: (0, 0, 0))],
        out_specs=pl.BlockSpec((rows, n_heads * n), lambda i: (0, 0)),
        out_shape=jax.ShapeDtypeStruct((rows, n_heads * n), BF16),
        compiler_params=_params("arbitrary"),
    )(x, w)


def _group_norm_gate(o, rg, gn_w):
    mu = jnp.mean(o, axis=-1, keepdims=True)
    d = o - mu
    var = jnp.mean(d * d, axis=-1, keepdims=True)
    on = d * lax.rsqrt(var + GN_EPS) * gn_w
    return (on * (rg * jax.nn.sigmoid(rg))).astype(BF16)


def _ret_prompt_kernel(q_ref, k_ref, v_ref, g_ref, cos_ref, sin_ref, lg_ref, gn_ref, o_ref, st_ref, state,
                       *, n_sub, chunk, k_scale):
    ci = pl.program_id(2)

    @pl.when(ci == 0)
    def _():
        state[...] = jnp.zeros(state.shape, F32)

    lg = lg_ref[0][:, :1]
    ii = lax.broadcasted_iota(jnp.int32, (chunk, chunk), 0)
    jj = lax.broadcasted_iota(jnp.int32, (chunk, chunk), 1)
    diff = (ii - jj).astype(F32)
    decay = jnp.where(diff >= 0, jnp.exp(jnp.maximum(diff, 0.0) * lg), 0.0)
    ri = lax.broadcasted_iota(jnp.int32, (chunk, 1), 0).astype(F32)
    xi = jnp.exp((ri + 1.0) * lg)
    to_end = jnp.exp((chunk - 1.0 - ri) * lg)
    g_chunk = jnp.exp(chunk * lg)
    gn_w = gn_ref[...]
    for sub in range(n_sub):
        sl = pl.ds(sub * chunk, chunk)
        cos = cos_ref[sl, :]
        sin = sin_ref[sl, :]
        q = _rope128(q_ref[sl, :], cos, sin)
        k = _rope128(k_ref[sl, :], cos, sin) * k_scale
        qb = q.astype(BF16)
        vb = v_ref[sl, :].astype(BF16)
        s = _dot_nt(qb, k.astype(BF16)) * decay
        st = state[...]
        o = _dot(s.astype(BF16), vb) + _dot(qb, st.astype(BF16)) * xi
        kw_t = jnp.transpose(k * to_end).astype(BF16)
        state[...] = g_chunk * st + _dot(kw_t, vb)
        o_ref[sl, :] = _group_norm_gate(o, g_ref[sl, :], gn_w)

    @pl.when(ci == pl.num_programs(2) - 1)
    def _():
        st_ref[0, 0] = state[...]


def _ret_prompt(z, cos128, sin128, lgam, gn_w, *, batch, seq, n_heads, dk, dv, rows, offs, total_rows):
    rq_off, rk_off, rv_off, rg_off = offs
    nr = seq // rows
    qk_spec = lambda off: pl.BlockSpec((rows, dk), lambda b, h, c: (b * nr + c, off // dk + h))
    v_spec = lambda off: pl.BlockSpec((rows, dv), lambda b, h, c: (b * nr + c, off // dv + h))
    tab = pl.BlockSpec((rows, LANES), lambda b, h, c: (b * nr + c, 0))
    return pl.pallas_call(
        functools.partial(_ret_prompt_kernel, n_sub=rows // RET_CHUNK, chunk=RET_CHUNK, k_scale=dk ** -0.5),
        grid=(batch, n_heads, nr),
        in_specs=[
            qk_spec(rq_off), qk_spec(rk_off), v_spec(rv_off), v_spec(rg_off), tab, tab,
            pl.BlockSpec((1, 1, LANES), lambda b, h, c: (h, 0, 0)),
            pl.BlockSpec((1, dv), lambda b, h, c: (0, h)),
        ],
        out_specs=[
            pl.BlockSpec((rows, dv), lambda b, h, c: (b * nr + c, h)),
            pl.BlockSpec((1, 1, dk, dv), lambda b, h, c: (b, h, 0, 0)),
        ],
        out_shape=[
            jax.ShapeDtypeStruct((total_rows, n_heads * dv), BF16),
            jax.ShapeDtypeStruct((batch, n_heads, dk, dv), F32),
        ],
        scratch_shapes=[pltpu.VMEM((dk, dv), F32)],
        compiler_params=_params("parallel", "parallel", "arbitrary"),
    )(z, z, z, z, cos128, sin128, lgam, gn_w)


def _ret_sample_kernel(q_ref, k_ref, v_ref, g_ref, cos_ref, sin_ref, lg_ref, gn_ref, st_in_ref, o_ref, st_out_ref,
                       *, nb, t_log2, k_scale):
    ts = 1 << t_log2
    rows = nb * ts
    lg = lg_ref[0][:, :1]
    ii = lax.broadcasted_iota(jnp.int32, (rows, rows), 0)
    jj = lax.broadcasted_iota(jnp.int32, (rows, rows), 1)
    diff = (ii - jj).astype(F32)
    keep = ((ii >> t_log2) == (jj >> t_log2)) & (ii >= jj)
    decay = jnp.where(keep, jnp.exp(jnp.maximum(diff, 0.0) * lg), 0.0)
    ri = lax.broadcasted_iota(jnp.int32, (rows, 1), 0)
    rb = ri >> t_log2
    ti = (ri & (ts - 1)).astype(F32)
    xi = jnp.exp((ti + 1.0) * lg)
    to_end = jnp.exp((ts - 1.0 - ti) * lg)
    g_chunk = jnp.exp(ts * lg)
    cos = cos_ref[...]
    sin = sin_ref[...]
    q = _rope128(q_ref[...], cos, sin)
    k = _rope128(k_ref[...], cos, sin) * k_scale
    qb = q.astype(BF16)
    vb = v_ref[...].astype(BF16)
    s = _dot_nt(qb, k.astype(BF16)) * decay
    o_in = _dot(s.astype(BF16), vb)
    kw = k * to_end
    o_cross = jnp.zeros(o_in.shape, F32)
    for b in range(nb):
        st = st_in_ref[b, 0]
        mine = rb == b
        o_cross = jnp.where(mine, _dot(qb, st.astype(BF16)), o_cross)
        kw_t = jnp.transpose(jnp.where(mine, kw, 0.0)).astype(BF16)
        st_out_ref[b, 0] = g_chunk * st + _dot(kw_t, vb)
    o_ref[...] = _group_norm_gate(o_in + o_cross * xi, g_ref[...], gn_ref[...])


def _ret_sample(z, cos128, sin128, lgam, gn_w, state, *, row0, dec_batch, dec_seq, n_heads, dk, dv, offs):
    rq_off, rk_off, rv_off, rg_off = offs
    t_log2 = dec_seq.bit_length() - 1
    assert 1 << t_log2 == dec_seq
    rows = LANES
    nb = rows // dec_seq
    nblk = dec_batch // nb
    rb0 = row0 // rows
    qk_spec = lambda off: pl.BlockSpec((rows, dk), lambda i, h: (rb0 + i, off // dk + h))
    v_spec = lambda off: pl.BlockSpec((rows, dv), lambda i, h: (rb0 + i, off // dv + h))
    tab = pl.BlockSpec((rows, LANES), lambda i, h: (rb0 + i, 0))
    return pl.pallas_call(
        functools.partial(_ret_sample_kernel, nb=nb, t_log2=t_log2, k_scale=dk ** -0.5),
        grid=(nblk, n_heads),
        in_specs=[
            qk_spec(rq_off), qk_spec(rk_off), v_spec(rv_off), v_spec(rg_off), tab, tab,
            pl.BlockSpec((1, 1, LANES), lambda i, h: (h, 0, 0)),
            pl.BlockSpec((1, dv), lambda i, h: (0, h)),
            pl.BlockSpec((nb, 1, dk, dv), lambda i, h: (i, h, 0, 0)),
        ],
        out_specs=[
            pl.BlockSpec((rows, dv), lambda i, h: (i, h)),
            pl.BlockSpec((nb, 1, dk, dv), lambda i, h: (i, h, 0, 0)),
        ],
        out_shape=[
            jax.ShapeDtypeStruct((dec_batch * dec_seq, n_heads * dv), BF16),
            jax.ShapeDtypeStruct(state.shape, F32),
        ],
        compiler_params=_params("parallel", "parallel"),
    )(z, z, z, z, cos128, sin128, lgam, gn_w, state)


def _softmax_rows(s):
    e = jnp.exp(s - jnp.max(s, axis=-1, keepdims=True))
    return e / jnp.sum(e, axis=-1, keepdims=True)


def _mem_attn_prompt_kernel(q_ref, k_ref, v_ref, o_ref, *, scale):
    s = _dot_nt(q_ref[...].astype(BF16), k_ref[...].astype(BF16)) * scale
    o_ref[...] = _dot(_softmax_rows(s).astype(BF16), v_ref[...].astype(BF16)).astype(BF16)


def _mem_attn_prompt(z, kv, *, batch, seq, mem_tokens, n_heads, hd, tq, mq_off, total_rows):
    nq = seq // tq
    return pl.pallas_call(
        functools.partial(_mem_attn_prompt_kernel, scale=hd ** -0.5),
        grid=(batch, n_heads, nq),
        in_specs=[
            pl.BlockSpec((tq, hd), lambda b, h, i: (b * nq + i, mq_off // hd + h)),
            pl.BlockSpec((mem_tokens, hd), lambda b, h, i: (b, h)),
            pl.BlockSpec((mem_tokens, hd), lambda b, h, i: (b, n_heads + h)),
        ],
        out_specs=pl.BlockSpec((tq, hd), lambda b, h, i: (b * nq + i, h)),
        out_shape=jax.ShapeDtypeStruct((total_rows, n_heads * hd), BF16),
        compiler_params=_params("parallel", "parallel", "parallel"),
    )(z, kv, kv)


def _mem_attn_sample_kernel(q_ref, k_ref, v_ref, o_ref, *, nb, n_heads, scale):
    rows = q_ref.shape[1]
    cols = k_ref.shape[1] * n_heads
    hd = k_ref.shape[3]
    row_h = lax.broadcasted_iota(jnp.int32, (rows, cols), 0) & (n_heads - 1)
    col_h = lax.broadcasted_iota(jnp.int32, (rows, cols), 1) & (n_heads - 1)
    own = row_h == col_h
    for b in range(nb):
        k = k_ref[b].reshape(cols, hd).astype(BF16)
        v = v_ref[b].reshape(cols, hd).astype(BF16)
        s = jnp.where(own, _dot_nt(q_ref[b].astype(BF16), k) * scale, NEG_BIG)
        o_ref[b] = _dot(_softmax_rows(s).astype(BF16), v).astype(BF16)


def _mem_attn_sample(mq, mem_k, mem_v, *, nb):
    dec_batch, rows, hd = mq.shape
    _, mem_tokens, n_heads, _ = mem_k.shape
    assert n_heads & (n_heads - 1) == 0
    cache = pl.BlockSpec((nb, mem_tokens, n_heads, hd), lambda i: (i, 0, 0, 0))
    return pl.pallas_call(
        functools.partial(_mem_attn_sample_kernel, nb=nb, n_heads=n_heads, scale=hd ** -0.5),
        grid=(dec_batch // nb,),
        in_specs=[pl.BlockSpec((nb, rows, hd), lambda i: (i, 0, 0)), cache, cache],
        out_specs=pl.BlockSpec((nb, rows, hd), lambda i: (i, 0, 0)),
        out_shape=jax.ShapeDtypeStruct((dec_batch, rows, hd), BF16),
        compiler_params=_params("parallel"),
    )(mq, mem_k, mem_v)


def _merge_kernel(oap_ref, obp_ref, ocp_ref, oas_ref, obs_ref, ocs_ref, wa_ref, wb_ref, wc_ref, ga_ref, gb_ref,
                  gc_ref, m_ref, *, n_first):
    def run(oa_ref, ob_ref, oc_ref):
        merged = jax.nn.sigmoid(ga_ref[...]) * _dot(oa_ref[...], wa_ref[...])
        merged = merged + jax.nn.sigmoid(gb_ref[...]) * _dot(ob_ref[...], wb_ref[...])
        merged = merged + jax.nn.sigmoid(gc_ref[...]) * _dot(oc_ref[...], wc_ref[...])
        m_ref[...] = merged.astype(BF16)

    i = pl.program_id(1)

    @pl.when(i < n_first)
    def _():
        run(oap_ref, obp_ref, ocp_ref)

    @pl.when(i >= n_first)
    def _():
        run(oas_ref, obs_ref, ocs_ref)


def _merge(prompt_outs, sample_outs, w_a, w_b, w_c, z, *, tm, tn, gz_off):
    tp = prompt_outs[0].shape[0]
    t = tp + sample_outs[0].shape[0]
    n_first = tp // tm
    d = w_a.shape[1]
    nj = d // tn
    act_p = lambda a: pl.BlockSpec((tm, a.shape[1]), lambda j, i: (jnp.minimum(i, n_first - 1), 0))
    act_s = lambda a: pl.BlockSpec((tm, a.shape[1]), lambda j, i: (jnp.maximum(i - n_first, 0), 0))
    wsp = lambda w: pl.BlockSpec((w.shape[0], tn), lambda j, i: (0, j))
    gate = lambda br: pl.BlockSpec((tm, tn), lambda j, i: (i, (gz_off + br * d) // tn + j))
    return pl.pallas_call(
        functools.partial(_merge_kernel, n_first=n_first),
        grid=(nj, t // tm),
        in_specs=[act_p(a) for a in prompt_outs] + [act_s(a) for a in sample_outs]
        + [wsp(w_a), wsp(w_b), wsp(w_c), gate(0), gate(1), gate(2)],
        out_specs=pl.BlockSpec((tm, tn), lambda j, i: (i, j)),
        out_shape=jax.ShapeDtypeStruct((t, d), BF16),
        compiler_params=_params("parallel", "parallel"),
    )(*prompt_outs, *sample_outs, w_a, w_b, w_c, z, z, z)


def _out_proj_kernel(x_ref, m_ref, wo_ref, nw_ref, wrh_ref, wrl_ref, br_ref, x1_ref, h_ref, lg_ref):
    x1 = x_ref[...] + _dot(m_ref[...], wo_ref[...])
    x1_ref[...] = x1
    h = _rms(x1, nw_ref[...])
    h_ref[...] = h
    h_hi = h.astype(BF16)
    h_lo = (h - h_hi.astype(F32)).astype(BF16)
    w_hi = wrh_ref[...]
    lg_ref[...] = _dot(h_hi, w_hi) + _dot(h_lo, w_hi) + _dot(h_hi, wrl_ref[...]) + br_ref[...]


def _out_proj(x, merged, w_o, norm_w, w_router_hi, w_router_lo, b_router, *, tm):
    t, d = x.shape
    row = lambda i: (i, 0)
    const = lambda i: (0, 0)
    return pl.pallas_call(
        _out_proj_kernel,
        grid=(t // tm,),
        in_specs=[
            pl.BlockSpec((tm, d), row),
            pl.BlockSpec((tm, d), row),
            pl.BlockSpec((d, d), const),
            pl.BlockSpec((1, d), const),
            pl.BlockSpec((d, LANES), const),
            pl.BlockSpec((d, LANES), const),
            pl.BlockSpec((1, LANES), const),
        ],
        out_specs=[pl.BlockSpec((tm, d), row), pl.BlockSpec((tm, d), row), pl.BlockSpec((tm, LANES), row)],
        out_shape=[
            jax.ShapeDtypeStruct((t, d), F32),
            jax.ShapeDtypeStruct((t, d), F32),
            jax.ShapeDtypeStruct((t, LANES), F32),
        ],
        compiler_params=_params("parallel"),
    )(x, merged, w_o, norm_w, w_router_hi, w_router_lo, b_router)


def _gather_rows(src_hbm, idx_ref, base, buf, sem, n_rows):
    def issue(r, carry):
        pltpu.make_async_copy(src_hbm.at[pl.ds(idx_ref[base + r], 1), :], buf.at[pl.ds(r, 1), :], sem).start()
        return carry

    lax.fori_loop(0, n_rows, issue, 0, unroll=8)


def _wait_rows(src_hbm, buf, sem, n_rows):
    pltpu.make_async_copy(src_hbm.at[pl.ds(0, n_rows), :], buf, sem).wait()


def _start_rows(src_hbm, idx_ref, base, buf, sem, n_rows):
    for r in range(n_rows):
        pltpu.make_async_copy(src_hbm.at[pl.ds(idx_ref[base + r], 1), :], buf.at[pl.ds(r, 1), :], sem).start()


def _experts_kernel(be_ref, nu_ref, tok_ref, h_hbm, wg_ref, wu_ref, wd_ref, y_ref, buf0, buf1, buf2, wg_b, wu_b,
                    wd_b, sems, *, blk):
    i = pl.program_id(0)
    n_used = nu_ref[0]
    bufs = (buf0, buf1, buf2)

    @pl.when(i == 0)
    def _():
        _gather_rows(h_hbm, tok_ref, 0, buf0, sems.at[0], blk)
        _gather_rows(h_hbm, tok_ref, jnp.minimum(1, n_used - 1) * blk, buf1, sems.at[1], blk)

    @pl.when((i < n_used) & ((i == 0) | (be_ref[i] != be_ref[jnp.maximum(i - 1, 0)])))
    def _():
        wg_b[...] = wg_ref[...].astype(BF16)
        wu_b[...] = wu_ref[...].astype(BF16)
        wd_b[...] = wd_ref[...].astype(BF16)

    def run(k):
        cur, mid, nxt = bufs[k], bufs[(k + 1) % 3], bufs[(k + 2) % 3]
        _wait_rows(h_hbm, cur, sems.at[k], blk)
        _start_rows(h_hbm, tok_ref, jnp.minimum(i + 2, n_used - 1) * blk, nxt, sems.at[(k + 2) % 3], blk)
        xb = cur[...].astype(BF16)
        g = _dot(xb, wg_b[...])
        u = _dot(xb, wu_b[...])
        a = (g * jax.nn.sigmoid(g) * u).astype(BF16)
        y_ref[...] = _dot(a, wd_b[...])

        @pl.when(i == n_used - 1)
        def _():
            _wait_rows(h_hbm, mid, sems.at[(k + 1) % 3], blk)
            _wait_rows(h_hbm, nxt, sems.at[(k + 2) % 3], blk)

    for k in range(3):
        @pl.when((i < n_used) & (i % 3 == k))
        def _():
            run(k)

    @pl.when(i >= n_used)
    def _():
        y_ref[...] = jnp.zeros(y_ref.shape, F32)


def _experts(blk_expert, n_used, row_tok, h, w_gate, w_up, w_down, *, n_blk, blk):
    d = h.shape[1]
    de = w_gate.shape[2]
    grid_spec = pltpu.PrefetchScalarGridSpec(
        num_scalar_prefetch=3,
        grid=(n_blk,),
        in_specs=[
            pl.BlockSpec(memory_space=pl.ANY),
            pl.BlockSpec((None, d, de), lambda i, be, nu, tok: (be[i], 0, 0)),
            pl.BlockSpec((None, d, de), lambda i, be, nu, tok: (be[i], 0, 0)),
            pl.BlockSpec((None, de, d), lambda i, be, nu, tok: (be[i], 0, 0)),
        ],
        out_specs=pl.BlockSpec((blk, d), lambda i, be, nu, tok: (i, 0)),
        scratch_shapes=[
            pltpu.VMEM((blk, d), F32),
            pltpu.VMEM((blk, d), F32),
            pltpu.VMEM((blk, d), F32),
            pltpu.VMEM((d, de), BF16),
            pltpu.VMEM((d, de), BF16),
            pltpu.VMEM((de, d), BF16),
            pltpu.SemaphoreType.DMA((3,)),
        ],
    )
    return pl.pallas_call(
        functools.partial(_experts_kernel, blk=blk),
        grid_spec=grid_spec,
        out_shape=jax.ShapeDtypeStruct((n_blk * blk, d), F32),
        compiler_params=_params("arbitrary"),
    )(blk_expert, n_used, row_tok, h, w_gate, w_up, w_down)


def _combine_kernel(d0_ref, d1_ref, y_hbm, x1_ref, g0_ref, g1_ref, nw_ref, op_ref, os_ref, b00, b01, b02, b10, b11,
                    b12, sems, *, tm, n_first):
    i = pl.program_id(0)
    last = pl.num_programs(0) - 1
    bufs0 = (b00, b01, b02)
    bufs1 = (b10, b11, b12)

    def start(step, k, straight):
        issue = _start_rows if straight else _gather_rows
        issue(y_hbm, d0_ref, step * tm, bufs0[k], sems.at[0, k], tm)
        issue(y_hbm, d1_ref, step * tm, bufs1[k], sems.at[1, k], tm)

    def wait(k):
        _wait_rows(y_hbm, bufs0[k], sems.at[0, k], tm)
        _wait_rows(y_hbm, bufs1[k], sems.at[1, k], tm)

    @pl.when(i == 0)
    def _():
        start(0, 0, False)
        start(jnp.minimum(1, last), 1, False)

    def run(k):
        wait(k)
        start(jnp.minimum(i + 2, last), (k + 2) % 3, True)
        reps = x1_ref.shape[1] // LANES
        g0 = jnp.tile(g0_ref[...], (1, reps))
        g1 = jnp.tile(g1_ref[...], (1, reps))
        x2 = x1_ref[...] + (bufs0[k][...] * g0 + bufs1[k][...] * g1)
        y = _rms(x2, nw_ref[...])

        @pl.when(i < n_first)
        def _():
            op_ref[...] = y

        @pl.when(i >= n_first)
        def _():
            os_ref[...] = y

        @pl.when(i == last)
        def _():
            wait((k + 1) % 3)
            wait((k + 2) % 3)

    for k in range(3):
        @pl.when(i % 3 == k)
        def _():
            run(k)


def _combine(d0, d1, y_rows, x1, g0, g1, norm_w, *, tm, rows_first):
    t, d = x1.shape
    n_first = rows_first // tm
    row = lambda i, a, b: (i, 0)
    grid_spec = pltpu.PrefetchScalarGridSpec(
        num_scalar_prefetch=2,
        grid=(t // tm,),
        in_specs=[
            pl.BlockSpec(memory_space=pl.ANY),
            pl.BlockSpec((tm, d), row),
            pl.BlockSpec((tm, LANES), row),
            pl.BlockSpec((tm, LANES), row),
            pl.BlockSpec((1, d), lambda i, a, b: (0, 0)),
        ],
        out_specs=[
            pl.BlockSpec((tm, d), lambda i, a, b: (jnp.minimum(i, n_first - 1), 0)),
            pl.BlockSpec((tm, d), lambda i, a, b: (jnp.maximum(i - n_first, 0), 0)),
        ],
        scratch_shapes=[pltpu.VMEM((tm, d), F32)] * 6 + [pltpu.SemaphoreType.DMA((2, 3))],
    )
    return pl.pallas_call(
        functools.partial(_combine_kernel, tm=tm, n_first=n_first),
        grid_spec=grid_spec,
        out_shape=[jax.ShapeDtypeStruct((rows_first, d), F32), jax.ShapeDtypeStruct((t - rows_first, d), F32)],
        compiler_params=_params("arbitrary"),
    )(d0, d1, y_rows, x1, g0, g1, norm_w)


def _route(logits, n_groups, per_group, top_k, blk):
    t = logits.shape[0]
    n_exp = n_groups * per_group
    assert top_k == 2
    grp_p = jax.nn.softmax(logits[:, :n_groups], axis=-1)
    g_idx = jnp.argmax(grp_p, axis=-1, keepdims=True)
    g_prob = jnp.max(grp_p, axis=-1, keepdims=True)
    e_logit = logits[:, n_groups:n_groups + n_exp].reshape(t, n_groups, per_group)
    in_grp = jnp.take_along_axis(e_logit, g_idx[:, :, None], axis=1)[:, 0, :]
    i1 = jnp.argmax(in_grp, axis=-1, keepdims=True)
    rest = jnp.where(jnp.arange(per_group)[None, :] == i1, -jnp.inf, in_grp)
    i2 = jnp.argmax(rest, axis=-1, keepdims=True)
    top_i = jnp.concatenate([i1, i2], axis=-1)
    top_v = jnp.concatenate([jnp.max(in_grp, axis=-1, keepdims=True), jnp.max(rest, axis=-1, keepdims=True)], axis=-1)
    gate = g_prob * jax.nn.softmax(top_v, axis=-1)
    expert = (g_idx * per_group + top_i).reshape(-1).astype(jnp.int32)
    a = t * top_k
    onehot = expert[:, None] == jnp.arange(n_exp, dtype=jnp.int32)[None, :]
    csum = jnp.cumsum(onehot.astype(jnp.int32), axis=0)
    counts = csum[-1]
    rank = jnp.sum(jnp.where(onehot, csum - 1, 0), axis=1)
    padded = (counts + blk - 1) // blk * blk
    pad_end = jnp.cumsum(padded)
    pad_start = pad_end - padded
    dest = (pad_start[expert] + rank).astype(jnp.int32)
    n_blk = (a + n_exp * (blk - 1) + blk - 1) // blk
    tok = jnp.arange(a, dtype=jnp.int32) // top_k
    row_tok = jnp.zeros((n_blk * blk,), jnp.int32).at[dest].set(tok)
    blk_start = jnp.arange(n_blk, dtype=jnp.int32) * blk
    blk_expert = jnp.minimum(jnp.sum(pad_end[None, :] <= blk_start[:, None], axis=1), n_exp - 1).astype(jnp.int32)
    n_used = (pad_end[-1:] // blk).astype(jnp.int32)
    return gate, dest.reshape(t, top_k), row_tok, blk_expert, n_used, n_blk


def _pick(n, *prefs):
    for p in prefs:
        if n % p == 0:
            return p
    return n


def kernel(x_prompt, x_sample, mem_prompt, cache_ckv, cache_krope, state_ret, cache_mem_k, cache_mem_v, page_table, attn_norm_w, w_in, mla_q_norm_w, mla_w_uq, mla_kv_norm_w, mla_w_uk, mla_w_uv, ret_gn_w, mem_norm_w, mem_w_kv, w_branch_a, w_branch_b, w_branch_c, w_out, ffn_norm_w, router_grp_w, router_grp_b, router_exp_w, router_exp_b, exp_w_gate, exp_w_up, exp_w_down, final_norm_w):
    batch, seq, d = x_prompt.shape
    dec_batch, dec_seq, _ = x_sample.shape
    depth = w_in.shape[0]
    q_rank, n_heads, qh = mla_w_uq.shape[1:]
    kv_rank, _, nope = mla_w_uk.shape[1:]
    rope_dim = qh - nope
    v_dim = mla_w_uv.shape[3]
    ret_heads, ret_dv = ret_gn_w.shape[1:]
    ret_dk = state_ret.shape[3]
    mem_tokens = mem_prompt.shape[1]
    mem_heads, mem_hd = cache_mem_k.shape[3:]
    mem_width = mem_heads * mem_hd
    n_groups = router_grp_w.shape[2]
    n_exp = router_exp_w.shape[2]
    per_group = n_exp // n_groups
    top_k = 2
    page = cache_ckv.shape[2]
    past_len = page_table.shape[1] * page
    assert nope == LANES and v_dim == LANES and rope_dim == 64 and ret_dk == LANES
    assert q_rank == kv_rank and n_groups + n_exp <= LANES

    tp = batch * seq
    ts = dec_batch * dec_seq
    t = tp + ts
    ret_qk = ret_heads * ret_dk
    ret_v = ret_heads * ret_dv
    cq_off, ckv_off = 0, q_rank
    rq_off = ckv_off + kv_rank
    rk_off = rq_off + ret_qk
    rv_off = rk_off + ret_qk
    rg_off = rv_off + ret_v
    mq_off = rg_off + ret_v
    gz_off = mq_off + mem_width
    ret_offs = (rq_off, rk_off, rv_off, rg_off)

    pos = jnp.concatenate([jnp.tile(jnp.arange(seq), batch), jnp.tile(past_len + jnp.arange(dec_seq), dec_batch)])
    cos64, sin64 = _rope_tables(pos, rope_dim)
    cos128, sin128 = _rope_tables(pos, ret_dk)
    lgam = jnp.log1p(-jnp.exp2(-5.0 - jnp.arange(ret_heads, dtype=F32)))
    lgam = jnp.broadcast_to(lgam[:, None, None], (ret_heads, 1, LANES))

    xp = x_prompt.reshape(tp, d)
    xs = x_sample.reshape(ts, d)
    x = jnp.concatenate([xp, xs], axis=0)

    assert depth == 1, "the fused final norm assumes a single layer"
    l = 0
    tm_mid = _pick(t, 256)

    wi = w_in[l]
    kr_lo = q_rank + kv_rank
    w_main = jnp.concatenate([wi[:, :kr_lo], wi[:, kr_lo + rope_dim:]], axis=1).astype(BF16)
    w_kr = jnp.pad(wi[:, kr_lo:kr_lo + rope_dim], ((0, 0), (0, LANES - rope_dim))).astype(BF16)
    z, zk = _norm_proj(x, attn_norm_w[l], w_main, w_kr, tm=_pick(t, 1536, 1024, 512),
                       tn=_pick(w_main.shape[1], 512))

    wuq = mla_w_uq[l]
    wqn = wuq[:, :, :nope].reshape(q_rank, n_heads * LANES).astype(BF16)
    wqr = jnp.pad(wuq[:, :, nope:], ((0, 0), (0, 0), (0, LANES - rope_dim))).reshape(q_rank, n_heads * LANES).astype(BF16)
    wuk = mla_w_uk[l].reshape(kv_rank, n_heads * nope).astype(BF16)
    wuv = mla_w_uv[l].reshape(kv_rank, n_heads * v_dim).astype(BF16)
    qcat, c, kr, kcat, v = _mla_prep(
        z, zk, cos64, sin64, mla_q_norm_w[l].reshape(1, -1), mla_kv_norm_w[l].reshape(1, -1), wqn, wqr, wuk, wuv,
        tm=tm_mid, n_heads=n_heads, q_rank=q_rank, kv_rank=kv_rank, rope_dim=rope_dim)

    scale = qh ** -0.5
    oa_p = _mla_prompt(qcat, kcat, v, batch=batch, seq=seq, n_heads=n_heads, tq=_pick(seq, 512), scale=scale,
                       total_rows=tp)
    wukt = jnp.transpose(mla_w_uk[l], (1, 2, 0)).astype(BF16)
    qlat, qr = _absorb_q(qcat, wukt, ts=ts, row_block=tp // ts, n_heads=n_heads, kv_rank=kv_rank)
    rows = dec_seq * n_heads
    new_pad = LANES - dec_seq
    cnew = jnp.pad(c[tp:].reshape(dec_batch, dec_seq, kv_rank), ((0, 0), (0, new_pad), (0, 0))).astype(BF16)
    krnew = jnp.pad(kr[tp:].reshape(dec_batch, dec_seq, rope_dim), ((0, 0), (0, new_pad), (0, 0))).astype(BF16)
    o_lat = _mla_decode(
        page_table, qlat.reshape(dec_batch, rows, kv_rank), qr.reshape(dec_batch, rows, LANES), cnew, krnew,
        cache_ckv[l], jnp.swapaxes(cache_krope[l], 1, 2), npg=_pick(page_table.shape[1], 32), scale=scale,
        n_heads=n_heads, rope_dim=rope_dim)
    wuv_h = jnp.transpose(mla_w_uv[l], (1, 0, 2)).astype(BF16)
    oa_s = _head_proj(o_lat.reshape(ts, n_heads * kv_rank), wuv_h, n_heads=n_heads)

    gn_w = ret_gn_w[l].reshape(1, ret_v)
    ob_p, st_p = _ret_prompt(z, cos128, sin128, lgam, gn_w, batch=batch, seq=seq, n_heads=ret_heads, dk=ret_dk,
                             dv=ret_dv, rows=_pick(seq, 1024, 512), offs=ret_offs, total_rows=tp)
    ob_s, st_s = _ret_sample(z, cos128, sin128, lgam, gn_w, state_ret[l], row0=tp, dec_batch=dec_batch,
                             dec_seq=dec_seq, n_heads=ret_heads, dk=ret_dk, dv=ret_dv, offs=ret_offs)

    mem_rows = batch * mem_tokens
    (kv,) = _norm_proj(mem_prompt.reshape(mem_rows, d), mem_norm_w[l], mem_w_kv[l].astype(BF16),
                       tm=_pick(mem_rows, 1024, 512), tn=_pick(2 * mem_width, 512))
    oc_p = _mem_attn_prompt(z, kv, batch=batch, seq=seq, mem_tokens=mem_tokens, n_heads=mem_heads, hd=mem_hd,
                            tq=_pick(seq, 512), mq_off=mq_off, total_rows=tp)
    mq_s = z[tp:, mq_off:mq_off + mem_width].reshape(dec_batch, dec_seq * mem_heads, mem_hd)
    oc_s = _mem_attn_sample(mq_s, cache_mem_k[l], cache_mem_v[l], nb=4).reshape(ts, mem_width)

    tm_merge = _pick(ts, 512, 256, 128)
    assert tp % tm_merge == 0
    merged = _merge((oa_p, ob_p, oc_p), (oa_s, ob_s, oc_s), w_branch_a[l].astype(BF16), w_branch_b[l].astype(BF16),
                    w_branch_c[l].astype(BF16), z, tm=tm_merge, tn=_pick(d, 1024), gz_off=gz_off)
    n_route = n_groups + n_exp
    w_router = jnp.pad(jnp.concatenate([router_grp_w[l], router_exp_w[l]], axis=1), ((0, 0), (0, LANES - n_route)))
    w_router_hi = w_router.astype(BF16)
    w_router_lo = (w_router - w_router_hi.astype(F32)).astype(BF16)
    b_router = jnp.pad(jnp.concatenate([router_grp_b[l], router_exp_b[l]]), (0, LANES - n_route)).reshape(1, LANES)
    x1, h2, logits = _out_proj(x, merged, w_out[l].astype(BF16), ffn_norm_w[l].reshape(1, d), w_router_hi,
                               w_router_lo, b_router, tm=tm_mid)

    gate, dest, row_tok, blk_expert, n_used, n_blk = _route(logits, n_groups, per_group, top_k, MOE_BLOCK)
    y_rows = _experts(blk_expert, n_used, row_tok, h2, exp_w_gate[l], exp_w_up[l], exp_w_down[l], n_blk=n_blk,
                      blk=MOE_BLOCK)
    g0 = jnp.broadcast_to(gate[:, 0:1], (t, LANES))
    g1 = jnp.broadcast_to(gate[:, 1:2], (t, LANES))
    tm_c = _pick(ts, 256, 128)
    assert tp % tm_c == 0
    y_p, y_s = _combine(dest[:, 0], dest[:, 1], y_rows, x1, g0, g1, final_norm_w.reshape(1, d), tm=tm_c, rows_first=tp)

    lead = lambda a: a[None]
    return (
        y_p.reshape(batch, seq, d),
        y_s.reshape(dec_batch, dec_seq, d),
        lead(c[:tp].reshape(batch, seq, kv_rank)),
        lead(kr[:tp].reshape(batch, seq, rope_dim)),
        lead(st_p),
        lead(kv[:, :mem_width].reshape(batch, mem_tokens, mem_heads, mem_hd)),
        lead(kv[:, mem_width:].reshape(batch, mem_tokens, mem_heads, mem_hd)),
        lead(c[tp:].reshape(dec_batch, dec_seq, kv_rank)),
        lead(kr[tp:].reshape(dec_batch, dec_seq, rope_dim)),
        lead(st_s),
    )
```

```python
import functools

import jax
import jax.numpy as jnp
from jax import lax
from jax.experimental import pallas as pl
from jax.experimental.pallas import tpu as pltpu

F32 = jnp.float32
BF16 = jnp.bfloat16

NORM_EPS = 1e-6
GN_EPS = 1e-5
ROPE_THETA = 10000.0
RET_CHUNK = 128
MOE_BLOCK = 128
DECODE_SLOTS = 3
LANES = 128
NEG_BIG = -1e30
LOG2_E = 1.4426950408889634
VMEM_LIMIT_BYTES = 56 * 1024 * 1024


def _params(*sem):
    return pltpu.CompilerParams(dimension_semantics=sem, vmem_limit_bytes=VMEM_LIMIT_BYTES)


def _dot(a, b):
    return jnp.dot(a, b, preferred_element_type=F32)


def _dot_nt(a, b):
    return lax.dot_general(a, b, (((1,), (1,)), ((), ())), preferred_element_type=F32)


def _rms(x, w):
    return x * lax.rsqrt(jnp.mean(x * x, axis=-1, keepdims=True) + NORM_EPS) * w


def _rope64(x, cos, sin):
    lane = lax.broadcasted_iota(jnp.int32, x.shape, 1)
    first = (lane & 63) < 32
    swapped = jnp.where(first, pltpu.roll(x, 96, 1), pltpu.roll(x, 32, 1))
    return x * cos + swapped * sin


def _rope128(x, cos, sin):
    return x * cos + pltpu.roll(x, 64, 1) * sin


def _rope_tables(pos, dim):
    half = dim // 2
    inv = ROPE_THETA ** (-jnp.arange(half, dtype=F32) / half)
    ang = pos.astype(F32)[:, None] * inv[None, :]
    cos = jnp.cos(ang)
    sin = jnp.sin(ang)
    c = jnp.concatenate([cos, cos], axis=-1)
    s = jnp.concatenate([-sin, sin], axis=-1)
    reps = LANES // dim
    return jnp.tile(c, (1, reps)), jnp.tile(s, (1, reps))


def _norm_proj_kernel(x_ref, nw_ref, w_ref, z_ref, h_ref):
    @pl.when(pl.program_id(1) == 0)
    def _():
        h_ref[...] = _rms(x_ref[...], nw_ref[...]).astype(BF16)

    z_ref[...] = _dot(h_ref[...], w_ref[...])


def _norm_proj_extra_kernel(x_ref, nw_ref, w_ref, we_ref, z_ref, ze_ref, h_ref):
    @pl.when(pl.program_id(1) == 0)
    def _():
        h = _rms(x_ref[...], nw_ref[...]).astype(BF16)
        h_ref[...] = h
        ze_ref[...] = _dot(h, we_ref[...])

    z_ref[...] = _dot(h_ref[...], w_ref[...])


def _norm_proj(x, norm_w, w, w_extra=None, *, tm, tn):
    t, d = x.shape
    n = w.shape[1]
    grid = (t // tm, n // tn)
    in_specs = [
        pl.BlockSpec((tm, d), lambda i, j: (i, 0)),
        pl.BlockSpec((1, d), lambda i, j: (0, 0)),
        pl.BlockSpec((d, tn), lambda i, j: (0, j)),
    ]
    out_specs = [pl.BlockSpec((tm, tn), lambda i, j: (i, j))]
    out_shape = [jax.ShapeDtypeStruct((t, n), F32)]
    args = [x, norm_w.reshape(1, d), w]
    body = _norm_proj_kernel
    if w_extra is not None:
        ne = w_extra.shape[1]
        in_specs.append(pl.BlockSpec((d, ne), lambda i, j: (0, 0)))
        out_specs.append(pl.BlockSpec((tm, ne), lambda i, j: (i, 0)))
        out_shape.append(jax.ShapeDtypeStruct((t, ne), F32))
        args.append(w_extra)
        body = _norm_proj_extra_kernel
    return pl.pallas_call(
        body,
        grid=grid,
        in_specs=in_specs,
        out_specs=out_specs,
        out_shape=out_shape,
        scratch_shapes=[pltpu.VMEM((tm, d), BF16)],
        compiler_params=_params("parallel", "arbitrary"),
    )(*args)


def _mla_prep_kernel(cq_ref, ckv_ref, zk_ref, cos_ref, sin_ref, qnw_ref, kvnw_ref, wqn_ref, wqr_ref,
                     wuk_ref, wuv_ref, qcat_ref, c_ref, kr_ref, kcat_ref, v_ref, *, n_heads, rope_dim):
    cos = cos_ref[...]
    sin = sin_ref[...]
    cqn = _rms(cq_ref[...], qnw_ref[...]).astype(BF16)
    qn = _dot(cqn, wqn_ref[...])
    qr = _dot(cqn, wqr_ref[...])
    c = _rms(ckv_ref[...], kvnw_ref[...])
    c_ref[...] = c
    cb = c.astype(BF16)
    kn = _dot(cb, wuk_ref[...])
    v_ref[...] = _dot(cb, wuv_ref[...]).astype(BF16)
    kr = _rope64(zk_ref[...], cos, sin)
    kr_ref[...] = kr[:, :rope_dim]
    krb = kr.astype(BF16)
    for h in range(n_heads):
        lo = h * 2 * LANES
        sl = slice(h * LANES, (h + 1) * LANES)
        qcat_ref[:, lo:lo + LANES] = qn[:, sl].astype(BF16)
        qcat_ref[:, lo + LANES:lo + 2 * LANES] = _rope64(qr[:, sl], cos, sin).astype(BF16)
        kcat_ref[:, lo:lo + LANES] = kn[:, sl].astype(BF16)
        kcat_ref[:, lo + LANES:lo + 2 * LANES] = krb


def _mla_prep(z, zk, cos64, sin64, qnw, kvnw, wqn, wqr, wuk, wuv, *, tm, n_heads, q_rank, kv_rank, rope_dim):
    t = z.shape[0]
    row = lambda i: (i, 0)
    const = lambda i: (0, 0)
    hw = n_heads * LANES
    return pl.pallas_call(
        functools.partial(_mla_prep_kernel, n_heads=n_heads, rope_dim=rope_dim),
        grid=(t // tm,),
        in_specs=[
            pl.BlockSpec((tm, q_rank), lambda i: (i, 0)),
            pl.BlockSpec((tm, kv_rank), lambda i: (i, q_rank // kv_rank)),
            pl.BlockSpec((tm, LANES), row),
            pl.BlockSpec((tm, LANES), row),
            pl.BlockSpec((tm, LANES), row),
            pl.BlockSpec((1, q_rank), const),
            pl.BlockSpec((1, kv_rank), const),
            pl.BlockSpec((q_rank, hw), const),
            pl.BlockSpec((q_rank, hw), const),
            pl.BlockSpec((kv_rank, hw), const),
            pl.BlockSpec((kv_rank, hw), const),
        ],
        out_specs=[
            pl.BlockSpec((tm, 2 * hw), row),
            pl.BlockSpec((tm, kv_rank), row),
            pl.BlockSpec((tm, rope_dim), row),
            pl.BlockSpec((tm, 2 * hw), row),
            pl.BlockSpec((tm, hw), row),
        ],
        out_shape=[
            jax.ShapeDtypeStruct((t, 2 * hw), BF16),
            jax.ShapeDtypeStruct((t, kv_rank), F32),
            jax.ShapeDtypeStruct((t, rope_dim), F32),
            jax.ShapeDtypeStruct((t, 2 * hw), BF16),
            jax.ShapeDtypeStruct((t, hw), BF16),
        ],
        compiler_params=_params("parallel"),
    )(z, z, zk, cos64, sin64, qnw, kvnw, wqn, wqr, wuk, wuv)


def _flash_kernel(q_ref, k_ref, v_ref, o_ref, *, tq, scale):
    qi = pl.program_id(2)
    q = q_ref[...]

    def block(j, carry, diagonal):
        m, l, acc = carry
        start = pl.multiple_of(j * tq, tq)
        k = k_ref[pl.ds(start, tq), :]
        v = v_ref[pl.ds(start, tq), :]
        s = _dot_nt(q, k) * (scale * LOG2_E)
        if diagonal:
            rows = lax.broadcasted_iota(jnp.int32, (tq, tq), 0)
            cols = lax.broadcasted_iota(jnp.int32, (tq, tq), 1)
            s = jnp.where(cols <= rows, s, NEG_BIG)
        m_new = jnp.maximum(m, jnp.max(s, axis=-1, keepdims=True))
        alpha = jnp.exp2(m - m_new)
        p = jnp.exp2(s - m_new)
        l = alpha * l + jnp.sum(p, axis=-1, keepdims=True)
        acc = alpha * acc + _dot(p.astype(BF16), v)
        return m_new, l, acc

    dv = v_ref.shape[-1]
    init = (jnp.full((tq, 1), NEG_BIG, F32), jnp.zeros((tq, 1), F32), jnp.zeros((tq, dv), F32))
    carry = lax.fori_loop(0, qi, functools.partial(block, diagonal=False), init)
    _, l, acc = block(qi, carry, True)
    o_ref[...] = (acc / l).astype(BF16)


def _mla_prompt(qcat, kcat, v, *, batch, seq, n_heads, tq, scale, total_rows):
    nq = seq // tq
    dv = v.shape[1] // n_heads
    return pl.pallas_call(
        functools.partial(_flash_kernel, tq=tq, scale=scale),
        grid=(batch, n_heads, nq),
        in_specs=[
            pl.BlockSpec((tq, 2 * LANES), lambda b, h, i: (b * nq + i, h)),
            pl.BlockSpec((seq, 2 * LANES), lambda b, h, i: (b, h)),
            pl.BlockSpec((seq, dv), lambda b, h, i: (b, h)),
        ],
        out_specs=pl.BlockSpec((tq, dv), lambda b, h, i: (b * nq + i, h)),
        out_shape=jax.ShapeDtypeStruct((total_rows, n_heads * dv), BF16),
        compiler_params=_params("parallel", "parallel", "arbitrary"),
    )(qcat, kcat, v)


def _absorb_q_kernel(qcat_ref, wukt_ref, qlat_ref, qr_ref, *, n_heads, kv_rank):
    for h in range(n_heads):
        lo = h * 2 * LANES
        qlat_ref[:, h * kv_rank:(h + 1) * kv_rank] = _dot(qcat_ref[:, lo:lo + LANES], wukt_ref[h]).astype(BF16)
        qr_ref[:, h * LANES:(h + 1) * LANES] = qcat_ref[:, lo + LANES:lo + 2 * LANES]


def _absorb_q(qcat, wukt, *, ts, row_block, n_heads, kv_rank):
    return pl.pallas_call(
        functools.partial(_absorb_q_kernel, n_heads=n_heads, kv_rank=kv_rank),
        grid=(1,),
        in_specs=[
            pl.BlockSpec((ts, qcat.shape[1]), lambda i: (row_block, 0)),
            pl.BlockSpec(wukt.shape, lambda i: (0, 0, 0)),
        ],
        out_specs=[
            pl.BlockSpec((ts, n_heads * kv_rank), lambda i: (0, 0)),
            pl.BlockSpec((ts, n_heads * LANES), lambda i: (0, 0)),
        ],
        out_shape=[
            jax.ShapeDtypeStruct((ts, n_heads * kv_rank), BF16),
            jax.ShapeDtypeStruct((ts, n_heads * LANES), BF16),
        ],
        compiler_params=_params("arbitrary"),
    )(qcat, wukt)


def _decode_kernel(pt_ref, qlat_ref, qr_ref, cnew_ref, krnew_ref, ckv_hbm, krt_hbm, o_ref, cbuf, kbuf, cb_ref, kb_ref,
                   m_ref, l_ref, acc_ref, sems, *, npg, scale, n_heads, rope_dim):
    b = pl.program_id(0)
    step = pl.program_id(1)
    steps = pl.num_programs(1)
    total = pl.num_programs(0) * steps
    n_slots = cbuf.shape[0]
    g_step = b * steps + step
    slot = g_step % n_slots
    page = cbuf.shape[2]

    def fetch(g):
        bb = g // steps
        ss = g % steps
        sl = g % n_slots

        def body(j, carry):
            pg = pt_ref[bb, ss * npg + j]
            pltpu.make_async_copy(ckv_hbm.at[pg], cbuf.at[sl, j], sems.at[0, sl]).start()
            pltpu.make_async_copy(krt_hbm.at[pg], kbuf.at[sl, j], sems.at[1, sl]).start()
            return carry

        lax.fori_loop(0, npg, body, 0, unroll=8)

    @pl.when(g_step == 0)
    def _():
        for g0 in range(n_slots - 1):
            @pl.when(g0 < total)
            def _():
                fetch(jnp.int32(g0))

    @pl.when(g_step + n_slots - 1 < total)
    def _():
        fetch(g_step + n_slots - 1)

    pltpu.make_async_copy(ckv_hbm.at[pl.ds(0, npg)], cbuf.at[slot], sems.at[0, slot]).wait()
    pltpu.make_async_copy(krt_hbm.at[pl.ds(0, npg)], kbuf.at[slot], sems.at[1, slot]).wait()

    @pl.when(step == 0)
    def _():
        m_ref[...] = jnp.full(m_ref.shape, NEG_BIG, F32)
        l_ref[...] = jnp.zeros(l_ref.shape, F32)
        acc_ref[...] = jnp.zeros(acc_ref.shape, F32)

    ql = qlat_ref[0]
    qr = qr_ref[0][:, :rope_dim]

    def update(s, vals):
        m_prev = m_ref[...]
        m_new = jnp.maximum(m_prev, jnp.max(s, axis=-1, keepdims=True))
        alpha = jnp.exp(m_prev - m_new)
        p = jnp.exp(s - m_new)
        l_ref[...] = alpha * l_ref[...] + jnp.sum(p, axis=-1, keepdims=True)
        acc_ref[...] = alpha * acc_ref[...] + _dot(p.astype(BF16), vals)
        m_ref[...] = m_new

    for j in range(npg):
        cb_ref[j * page:(j + 1) * page, :] = cbuf[slot, j].astype(BF16)
        kb_ref[:, j * page:(j + 1) * page] = kbuf[slot, j].astype(BF16)
    cb = cb_ref[...]
    update((_dot_nt(ql, cb) + _dot(qr, kb_ref[...])) * scale, cb)

    @pl.when(step == pl.num_programs(1) - 1)
    def _():
        cn = cnew_ref[0]
        krn = krnew_ref[0]
        s = (_dot_nt(ql, cn) + _dot_nt(qr, krn)) * scale
        t_row = lax.broadcasted_iota(jnp.int32, s.shape, 0) >> (n_heads.bit_length() - 1)
        col = lax.broadcasted_iota(jnp.int32, s.shape, 1)
        update(jnp.where(col <= t_row, s, NEG_BIG), cn)
        o_ref[0] = (acc_ref[...] / l_ref[...]).astype(BF16)


def _mla_decode(page_table, qlat, qr, cnew, krnew, cache_ckv, cache_krope_t, *, npg, scale, n_heads, rope_dim):
    nb, rows, kv_rank = qlat.shape
    n_pages = page_table.shape[1]
    page = cache_ckv.shape[1]
    steps = n_pages // npg
    per_b = lambda b, p, pt: (b, 0, 0)
    grid_spec = pltpu.PrefetchScalarGridSpec(
        num_scalar_prefetch=1,
        grid=(nb, steps),
        in_specs=[
            pl.BlockSpec((1, rows, kv_rank), per_b),
            pl.BlockSpec((1, rows, LANES), per_b),
            pl.BlockSpec((1,) + cnew.shape[1:], per_b),
            pl.BlockSpec((1,) + krnew.shape[1:], per_b),
            pl.BlockSpec(memory_space=pl.ANY),
            pl.BlockSpec(memory_space=pl.ANY),
        ],
        out_specs=pl.BlockSpec((1, rows, kv_rank), per_b),
        scratch_shapes=[
            pltpu.VMEM((DECODE_SLOTS, npg, page, kv_rank), F32),
            pltpu.VMEM((DECODE_SLOTS, npg, rope_dim, page), F32),
            pltpu.VMEM((npg * page, kv_rank), BF16),
            pltpu.VMEM((rope_dim, npg * page), BF16),
            pltpu.VMEM((rows, 1), F32),
            pltpu.VMEM((rows, 1), F32),
            pltpu.VMEM((rows, kv_rank), F32),
            pltpu.SemaphoreType.DMA((2, DECODE_SLOTS)),
        ],
    )
    return pl.pallas_call(
        functools.partial(_decode_kernel, npg=npg, scale=scale, n_heads=n_heads, rope_dim=rope_dim),
        grid_spec=grid_spec,
        out_shape=jax.ShapeDtypeStruct((nb, rows, kv_rank), BF16),
        compiler_params=_params("arbitrary", "arbitrary"),
    )(page_table, qlat, qr, cnew, krnew, cache_ckv, cache_krope_t)


def _head_proj_kernel(x_ref, w_ref, o_ref, *, n_heads):
    k = x_ref.shape[1] // n_heads
    n = o_ref.shape[1] // n_heads
    for h in range(n_heads):
        o_ref[:, h * n:(h + 1) * n] = _dot(x_ref[:, h * k:(h + 1) * k], w_ref[h]).astype(BF16)


def _head_proj(x, w, *, n_heads):
    rows = x.shape[0]
    n = w.shape[2]
    return pl.pallas_call(
        functools.partial(_head_proj_kernel, n_heads=n_heads),
        grid=(1,),
        in_specs=[pl.BlockSpec(x.shape, lambda i: (0, 0)), pl.BlockSpec(w.shape, lambda i: (0, 0, 0))],
        out_specs=pl.BlockSpec((rows, n_heads * n), lambda i: (0, 0)),
        out_shape=jax.ShapeDtypeStruct((rows, n_heads * n), BF16),
        compiler_params=_params("arbitrary"),
    )(x, w)


def _group_norm_gate(o, rg, gn_w):
    mu = jnp.mean(o, axis=-1, keepdims=True)
    d = o - mu
    var = jnp.mean(d * d, axis=-1, keepdims=True)
    on = d * lax.rsqrt(var + GN_EPS) * gn_w
    return (on * (rg * jax.nn.sigmoid(rg))).astype(BF16)


def _ret_prompt_kernel(q_ref, k_ref, v_ref, g_ref, cos_ref, sin_ref, lg_ref, gn_ref, o_ref, st_ref, state,
                       *, n_sub, chunk, k_scale):
    ci = pl.program_id(2)

    @pl.when(ci == 0)
    def _():
        state[...] = jnp.zeros(state.shape, F32)

    lg = lg_ref[0][:, :1]
    ii = lax.broadcasted_iota(jnp.int32, (chunk, chunk), 0)
    jj = lax.broadcasted_iota(jnp.int32, (chunk, chunk), 1)
    diff = (ii - jj).astype(F32)
    decay = jnp.where(diff >= 0, jnp.exp(jnp.maximum(diff, 0.0) * lg), 0.0)
    ri = lax.broadcasted_iota(jnp.int32, (chunk, 1), 0).astype(F32)
    xi = jnp.exp((ri + 1.0) * lg)
    to_end = jnp.exp((chunk - 1.0 - ri) * lg)
    g_chunk = jnp.exp(chunk * lg)
    gn_w = gn_ref[...]
    for sub in range(n_sub):
        sl = pl.ds(sub * chunk, chunk)
        cos = cos_ref[sl, :]
        sin = sin_ref[sl, :]
        q = _rope128(q_ref[sl, :], cos, sin)
        k = _rope128(k_ref[sl, :], cos, sin) * k_scale
        qb = q.astype(BF16)
        vb = v_ref[sl, :].astype(BF16)
        s = _dot_nt(qb, k.astype(BF16)) * decay
        st = state[...]
        o = _dot(s.astype(BF16), vb) + _dot(qb, st.astype(BF16)) * xi
        kw_t = jnp.transpose(k * to_end).astype(BF16)
        state[...] = g_chunk * st + _dot(kw_t, vb)
        o_ref[sl, :] = _group_norm_gate(o, g_ref[sl, :], gn_w)

    @pl.when(ci == pl.num_programs(2) - 1)
    def _():
        st_ref[0, 0] = state[...]


def _ret_prompt(z, cos128, sin128, lgam, gn_w, *, batch, seq, n_heads, dk, dv, rows, offs, total_rows):
    rq_off, rk_off, rv_off, rg_off = offs
    nr = seq // rows
    qk_spec = lambda off: pl.BlockSpec((rows, dk), lambda b, h, c: (b * nr + c, off // dk + h))
    v_spec = lambda off: pl.BlockSpec((rows, dv), lambda b, h, c: (b * nr + c, off // dv + h))
    tab = pl.BlockSpec((rows, LANES), lambda b, h, c: (b * nr + c, 0))
    return pl.pallas_call(
        functools.partial(_ret_prompt_kernel, n_sub=rows // RET_CHUNK, chunk=RET_CHUNK, k_scale=dk ** -0.5),
        grid=(batch, n_heads, nr),
        in_specs=[
            qk_spec(rq_off), qk_spec(rk_off), v_spec(rv_off), v_spec(rg_off), tab, tab,
            pl.BlockSpec((1, 1, LANES), lambda b, h, c: (h, 0, 0)),
            pl.BlockSpec((1, dv), lambda b, h, c: (0, h)),
        ],
        out_specs=[
            pl.BlockSpec((rows, dv), lambda b, h, c: (b * nr + c, h)),
            pl.BlockSpec((1, 1, dk, dv), lambda b, h, c: (b, h, 0, 0)),
        ],
        out_shape=[
            jax.ShapeDtypeStruct((total_rows, n_heads * dv), BF16),
            jax.ShapeDtypeStruct((batch, n_heads, dk, dv), F32),
        ],
        scratch_shapes=[pltpu.VMEM((dk, dv), F32)],
        compiler_params=_params("parallel", "parallel", "arbitrary"),
    )(z, z, z, z, cos128, sin128, lgam, gn_w)


def _ret_sample_kernel(q_ref, k_ref, v_ref, g_ref, cos_ref, sin_ref, lg_ref, gn_ref, st_in_ref, o_ref, st_out_ref,
                       *, nb, t_log2, k_scale):
    ts = 1 << t_log2
    rows = nb * ts
    lg = lg_ref[0][:, :1]
    ii = lax.broadcasted_iota(jnp.int32, (rows, rows), 0)
    jj = lax.broadcasted_iota(jnp.int32, (rows, rows), 1)
    diff = (ii - jj).astype(F32)
    keep = ((ii >> t_log2) == (jj >> t_log2)) & (ii >= jj)
    decay = jnp.where(keep, jnp.exp(jnp.maximum(diff, 0.0) * lg), 0.0)
    ri = lax.broadcasted_iota(jnp.int32, (rows, 1), 0)
    rb = ri >> t_log2
    ti = (ri & (ts - 1)).astype(F32)
    xi = jnp.exp((ti + 1.0) * lg)
    to_end = jnp.exp((ts - 1.0 - ti) * lg)
    g_chunk = jnp.exp(ts * lg)
    cos = cos_ref[...]
    sin = sin_ref[...]
    q = _rope128(q_ref[...], cos, sin)
    k = _rope128(k_ref[...], cos, sin) * k_scale
    qb = q.astype(BF16)
    vb = v_ref[...].astype(BF16)
    s = _dot_nt(qb, k.astype(BF16)) * decay
    o_in = _dot(s.astype(BF16), vb)
    kw = k * to_end
    o_cross = jnp.zeros(o_in.shape, F32)
    for b in range(nb):
        st = st_in_ref[b, 0]
        mine = rb == b
        o_cross = jnp.where(mine, _dot(qb, st.astype(BF16)), o_cross)
        kw_t = jnp.transpose(jnp.where(mine, kw, 0.0)).astype(BF16)
        st_out_ref[b, 0] = g_chunk * st + _dot(kw_t, vb)
    o_ref[...] = _group_norm_gate(o_in + o_cross * xi, g_ref[...], gn_ref[...])


def _ret_sample(z, cos128, sin128, lgam, gn_w, state, *, row0, dec_batch, dec_seq, n_heads, dk, dv, offs):
    rq_off, rk_off, rv_off, rg_off = offs
    t_log2 = dec_seq.bit_length() - 1
    assert 1 << t_log2 == dec_seq
    rows = LANES
    nb = rows // dec_seq
    nblk = dec_batch // nb
    rb0 = row0 // rows
    qk_spec = lambda off: pl.BlockSpec((rows, dk), lambda i, h: (rb0 + i, off // dk + h))
    v_spec = lambda off: pl.BlockSpec((rows, dv), lambda i, h: (rb0 + i, off // dv + h))
    tab = pl.BlockSpec((rows, LANES), lambda i, h: (rb0 + i, 0))
    return pl.pallas_call(
        functools.partial(_ret_sample_kernel, nb=nb, t_log2=t_log2, k_scale=dk ** -0.5),
        grid=(nblk, n_heads),
        in_specs=[
            qk_spec(rq_off), qk_spec(rk_off), v_spec(rv_off), v_spec(rg_off), tab, tab,
            pl.BlockSpec((1, 1, LANES), lambda i, h: (h, 0, 0)),
            pl.BlockSpec((1, dv), lambda i, h: (0, h)),
            pl.BlockSpec((nb, 1, dk, dv), lambda i, h: (i, h, 0, 0)),
        ],
        out_specs=[
            pl.BlockSpec((rows, dv), lambda i, h: (i, h)),
            pl.BlockSpec((nb, 1, dk, dv), lambda i, h: (i, h, 0, 0)),
        ],
        out_shape=[
            jax.ShapeDtypeStruct((dec_batch * dec_seq, n_heads * dv), BF16),
            jax.ShapeDtypeStruct(state.shape, F32),
        ],
        compiler_params=_params("parallel", "parallel"),
    )(z, z, z, z, cos128, sin128, lgam, gn_w, state)


def _softmax_rows(s):
    e = jnp.exp(s - jnp.max(s, axis=-1, keepdims=True))
    return e / jnp.sum(e, axis=-1, keepdims=True)


def _mem_attn_prompt_kernel(q_ref, k_ref, v_ref, o_ref, *, scale):
    s = _dot_nt(q_ref[...].astype(BF16), k_ref[...].astype(BF16)) * scale
    o_ref[...] = _dot(_softmax_rows(s).astype(BF16), v_ref[...].astype(BF16)).astype(BF16)


def _mem_attn_prompt(z, kv, *, batch, seq, mem_tokens, n_heads, hd, tq, mq_off, total_rows):
    nq = seq // tq
    return pl.pallas_call(
        functools.partial(_mem_attn_prompt_kernel, scale=hd ** -0.5),
        grid=(batch, n_heads, nq),
        in_specs=[
            pl.BlockSpec((tq, hd), lambda b, h, i: (b * nq + i, mq_off // hd + h)),
            pl.BlockSpec((mem_tokens, hd), lambda b, h, i: (b, h)),
            pl.BlockSpec((mem_tokens, hd), lambda b, h, i: (b, n_heads + h)),
        ],
        out_specs=pl.BlockSpec((tq, hd), lambda b, h, i: (b * nq + i, h)),
        out_shape=jax.ShapeDtypeStruct((total_rows, n_heads * hd), BF16),
        compiler_params=_params("parallel", "parallel", "parallel"),
    )(z, kv, kv)


def _mem_attn_sample_kernel(q_ref, k_ref, v_ref, o_ref, *, nb, n_heads, scale):
    rows = q_ref.shape[1]
    cols = k_ref.shape[1] * n_heads
    hd = k_ref.shape[3]
    row_h = lax.broadcasted_iota(jnp.int32, (rows, cols), 0) & (n_heads - 1)
    col_h = lax.broadcasted_iota(jnp.int32, (rows, cols), 1) & (n_heads - 1)
    own = row_h == col_h
    for b in range(nb):
        k = k_ref[b].reshape(cols, hd).astype(BF16)
        v = v_ref[b].reshape(cols, hd).astype(BF16)
        s = jnp.where(own, _dot_nt(q_ref[b].astype(BF16), k) * scale, NEG_BIG)
        o_ref[b] = _dot(_softmax_rows(s).astype(BF16), v).astype(BF16)


def _mem_attn_sample(mq, mem_k, mem_v, *, nb):
    dec_batch, rows, hd = mq.shape
    _, mem_tokens, n_heads, _ = mem_k.shape
    assert n_heads & (n_heads - 1) == 0
    cache = pl.BlockSpec((nb, mem_tokens, n_heads, hd), lambda i: (i, 0, 0, 0))
    return pl.pallas_call(
        functools.partial(_mem_attn_sample_kernel, nb=nb, n_heads=n_heads, scale=hd ** -0.5),
        grid=(dec_batch // nb,),
        in_specs=[pl.BlockSpec((nb, rows, hd), lambda i: (i, 0, 0)), cache, cache],
        out_specs=pl.BlockSpec((nb, rows, hd), lambda i: (i, 0, 0)),
        out_shape=jax.ShapeDtypeStruct((dec_batch, rows, hd), BF16),
        compiler_params=_params("parallel"),
    )(mq, mem_k, mem_v)


def _merge_kernel(oap_ref, obp_ref, ocp_ref, oas_ref, obs_ref, ocs_ref, wa_ref, wb_ref, wc_ref, ga_ref, gb_ref,
                  gc_ref, m_ref, *, n_first):
    def run(oa_ref, ob_ref, oc_ref):
        merged = jax.nn.sigmoid(ga_ref[...]) * _dot(oa_ref[...], wa_ref[...])
        merged = merged + jax.nn.sigmoid(gb_ref[...]) * _dot(ob_ref[...], wb_ref[...])
        merged = merged + jax.nn.sigmoid(gc_ref[...]) * _dot(oc_ref[...], wc_ref[...])
        m_ref[...] = merged.astype(BF16)

    i = pl.program_id(1)

    @pl.when(i < n_first)
    def _():
        run(oap_ref, obp_ref, ocp_ref)

    @pl.when(i >= n_first)
    def _():
        run(oas_ref, obs_ref, ocs_ref)


def _merge(prompt_outs, sample_outs, w_a, w_b, w_c, z, *, tm, tn, gz_off):
    tp = prompt_outs[0].shape[0]
    t = tp + sample_outs[0].shape[0]
    n_first = tp // tm
    d = w_a.shape[1]
    nj = d // tn
    act_p = lambda a: pl.BlockSpec((tm, a.shape[1]), lambda j, i: (jnp.minimum(i, n_first - 1), 0))
    act_s = lambda a: pl.BlockSpec((tm, a.shape[1]), lambda j, i: (jnp.maximum(i - n_first, 0), 0))
    wsp = lambda w: pl.BlockSpec((w.shape[0], tn), lambda j, i: (0, j))
    gate = lambda br: pl.BlockSpec((tm, tn), lambda j, i: (i, (gz_off + br * d) // tn + j))
    return pl.pallas_call(
        functools.partial(_merge_kernel, n_first=n_first),
        grid=(nj, t // tm),
        in_specs=[act_p(a) for a in prompt_outs] + [act_s(a) for a in sample_outs]
        + [wsp(w_a), wsp(w_b), wsp(w_c), gate(0), gate(1), gate(2)],
        out_specs=pl.BlockSpec((tm, tn), lambda j, i: (i, j)),
        out_shape=jax.ShapeDtypeStruct((t, d), BF16),
        compiler_params=_params("parallel", "parallel"),
    )(*prompt_outs, *sample_outs, w_a, w_b, w_c, z, z, z)


def _out_proj_kernel(x_ref, m_ref, wo_ref, nw_ref, wrh_ref, wrl_ref, br_ref, x1_ref, h_ref, lg_ref):
    x1 = x_ref[...] + _dot(m_ref[...], wo_ref[...])
    x1_ref[...] = x1
    h = _rms(x1, nw_ref[...])
    h_ref[...] = h
    h_hi = h.astype(BF16)
    h_lo = (h - h_hi.astype(F32)).astype(BF16)
    w_hi = wrh_ref[...]
    lg_ref[...] = _dot(h_hi, w_hi) + _dot(h_lo, w_hi) + _dot(h_hi, wrl_ref[...]) + br_ref[...]


def _out_proj(x, merged, w_o, norm_w, w_router_hi, w_router_lo, b_router, *, tm):
    t, d = x.shape
    row = lambda i: (i, 0)
    const = lambda i: (0, 0)
    return pl.pallas_call(
        _out_proj_kernel,
        grid=(t // tm,),
        in_specs=[
            pl.BlockSpec((tm, d), row),
            pl.BlockSpec((tm, d), row),
            pl.BlockSpec((d, d), const),
            pl.BlockSpec((1, d), const),
            pl.BlockSpec((d, LANES), const),
            pl.BlockSpec((d, LANES), const),
            pl.BlockSpec((1, LANES), const),
        ],
        out_specs=[pl.BlockSpec((tm, d), row), pl.BlockSpec((tm, d), row), pl.BlockSpec((tm, LANES), row)],
        out_shape=[
            jax.ShapeDtypeStruct((t, d), F32),
            jax.ShapeDtypeStruct((t, d), F32),
            jax.ShapeDtypeStruct((t, LANES), F32),
        ],
        compiler_params=_params("parallel"),
    )(x, merged, w_o, norm_w, w_router_hi, w_router_lo, b_router)


def _gather_rows(src_hbm, idx_ref, base, buf, sem, n_rows):
    def issue(r, carry):
        pltpu.make_async_copy(src_hbm.at[pl.ds(idx_ref[base + r], 1), :], buf.at[pl.ds(r, 1), :], sem).start()
        return carry

    lax.fori_loop(0, n_rows, issue, 0, unroll=8)


def _wait_rows(src_hbm, buf, sem, n_rows):
    pltpu.make_async_copy(src_hbm.at[pl.ds(0, n_rows), :], buf, sem).wait()


def _start_rows(src_hbm, idx_ref, base, buf, sem, n_rows, priority):
    for r in range(n_rows):
        copy = pltpu.make_async_copy(src_hbm.at[pl.ds(idx_ref[base + r], 1), :], buf.at[pl.ds(r, 1), :], sem)
        copy.start(priority=priority(r))


def _experts_kernel(be_ref, nu_ref, tok_ref, h_hbm, wg_ref, wu_ref, wd_ref, y_ref, buf0, buf1, buf2, wg_b, wu_b,
                    wd_b, sems, *, blk):
    i = pl.program_id(0)
    n_used = nu_ref[0]
    bufs = (buf0, buf1, buf2)

    @pl.when(i == 0)
    def _():
        _gather_rows(h_hbm, tok_ref, 0, buf0, sems.at[0], blk)
        _gather_rows(h_hbm, tok_ref, jnp.minimum(1, n_used - 1) * blk, buf1, sems.at[1], blk)

    @pl.when((i < n_used) & ((i == 0) | (be_ref[i] != be_ref[jnp.maximum(i - 1, 0)])))
    def _():
        wg_b[...] = wg_ref[...].astype(BF16)
        wu_b[...] = wu_ref[...].astype(BF16)
        wd_b[...] = wd_ref[...].astype(BF16)

    def run(k):
        cur, mid, nxt = bufs[k], bufs[(k + 1) % 3], bufs[(k + 2) % 3]
        _wait_rows(h_hbm, cur, sems.at[k], blk)
        _start_rows(h_hbm, tok_ref, jnp.minimum(i + 2, n_used - 1) * blk, nxt, sems.at[(k + 2) % 3], blk,
                    lambda r: 1)
        xb = cur[...].astype(BF16)
        g = _dot(xb, wg_b[...])
        u = _dot(xb, wu_b[...])
        a = (g * jax.nn.sigmoid(g) * u).astype(BF16)
        y_ref[...] = _dot(a, wd_b[...])

        @pl.when(i == n_used - 1)
        def _():
            _wait_rows(h_hbm, mid, sems.at[(k + 1) % 3], blk)
            _wait_rows(h_hbm, nxt, sems.at[(k + 2) % 3], blk)

    for k in range(3):
        @pl.when((i < n_used) & (i % 3 == k))
        def _():
            run(k)

    @pl.when(i >= n_used)
    def _():
        y_ref[...] = jnp.zeros(y_ref.shape, F32)


def _experts(blk_expert, n_used, row_tok, h, w_gate, w_up, w_down, *, n_blk, blk):
    d = h.shape[1]
    de = w_gate.shape[2]
    grid_spec = pltpu.PrefetchScalarGridSpec(
        num_scalar_prefetch=3,
        grid=(n_blk,),
        in_specs=[
            pl.BlockSpec(memory_space=pl.ANY),
            pl.BlockSpec((None, d, de), lambda i, be, nu, tok: (be[i], 0, 0)),
            pl.BlockSpec((None, d, de), lambda i, be, nu, tok: (be[i], 0, 0)),
            pl.BlockSpec((None, de, d), lambda i, be, nu, tok: (be[i], 0, 0)),
        ],
        out_specs=pl.BlockSpec((blk, d), lambda i, be, nu, tok: (i, 0)),
        scratch_shapes=[
            pltpu.VMEM((blk, d), F32),
            pltpu.VMEM((blk, d), F32),
            pltpu.VMEM((blk, d), F32),
            pltpu.VMEM((d, de), BF16),
            pltpu.VMEM((d, de), BF16),
            pltpu.VMEM((de, d), BF16),
            pltpu.SemaphoreType.DMA((3,)),
        ],
    )
    return pl.pallas_call(
        functools.partial(_experts_kernel, blk=blk),
        grid_spec=grid_spec,
        out_shape=jax.ShapeDtypeStruct((n_blk * blk, d), F32),
        compiler_params=_params("arbitrary"),
    )(blk_expert, n_used, row_tok, h, w_gate, w_up, w_down)


def _combine_kernel(d0_ref, d1_ref, y_hbm, x1_ref, g0_ref, g1_ref, nw_ref, op_ref, os_ref, b00, b01, b02, b10, b11,
                    b12, sems, *, tm, n_first):
    i = pl.program_id(0)
    last = pl.num_programs(0) - 1
    bufs0 = (b00, b01, b02)
    bufs1 = (b10, b11, b12)

    def start(step, k, straight):
        if straight:
            _start_rows(y_hbm, d0_ref, step * tm, bufs0[k], sems.at[0, k], tm, lambda r: r % 2)
            _start_rows(y_hbm, d1_ref, step * tm, bufs1[k], sems.at[1, k], tm, lambda r: (r + 1) % 2)
        else:
            _gather_rows(y_hbm, d0_ref, step * tm, bufs0[k], sems.at[0, k], tm)
            _gather_rows(y_hbm, d1_ref, step * tm, bufs1[k], sems.at[1, k], tm)

    def wait(k):
        _wait_rows(y_hbm, bufs0[k], sems.at[0, k], tm)
        _wait_rows(y_hbm, bufs1[k], sems.at[1, k], tm)

    @pl.when(i == 0)
    def _():
        start(0, 0, False)
        start(jnp.minimum(1, last), 1, False)

    def run(k):
        wait(k)
        start(jnp.minimum(i + 2, last), (k + 2) % 3, True)
        reps = x1_ref.shape[1] // LANES
        g0 = jnp.tile(g0_ref[...], (1, reps))
        g1 = jnp.tile(g1_ref[...], (1, reps))
        x2 = x1_ref[...] + (bufs0[k][...] * g0 + bufs1[k][...] * g1)
        y = _rms(x2, nw_ref[...])

        @pl.when(i < n_first)
        def _():
            op_ref[...] = y

        @pl.when(i >= n_first)
        def _():
            os_ref[...] = y

        @pl.when(i == last)
        def _():
            wait((k + 1) % 3)
            wait((k + 2) % 3)

    for k in range(3):
        @pl.when(i % 3 == k)
        def _():
            run(k)


def _combine(d0, d1, y_rows, x1, g0, g1, norm_w, *, tm, rows_first):
    t, d = x1.shape
    n_first = rows_first // tm
    row = lambda i, a, b: (i, 0)
    grid_spec = pltpu.PrefetchScalarGridSpec(
        num_scalar_prefetch=2,
        grid=(t // tm,),
        in_specs=[
            pl.BlockSpec(memory_space=pl.ANY),
            pl.BlockSpec((tm, d), row),
            pl.BlockSpec((tm, LANES), row),
            pl.BlockSpec((tm, LANES), row),
            pl.BlockSpec((1, d), lambda i, a, b: (0, 0)),
        ],
        out_specs=[
            pl.BlockSpec((tm, d), lambda i, a, b: (jnp.minimum(i, n_first - 1), 0)),
            pl.BlockSpec((tm, d), lambda i, a, b: (jnp.maximum(i - n_first, 0), 0)),
        ],
        scratch_shapes=[pltpu.VMEM((tm, d), F32)] * 6 + [pltpu.SemaphoreType.DMA((2, 3))],
    )
    return pl.pallas_call(
        functools.partial(_combine_kernel, tm=tm, n_first=n_first),
        grid_spec=grid_spec,
        out_shape=[jax.ShapeDtypeStruct((rows_first, d), F32), jax.ShapeDtypeStruct((t - rows_first, d), F32)],
        compiler_params=_params("arbitrary"),
    )(d0, d1, y_rows, x1, g0, g1, norm_w)


def _route(logits, n_groups, per_group, top_k, blk):
    t = logits.shape[0]
    n_exp = n_groups * per_group
    assert top_k == 2
    grp_p = jax.nn.softmax(logits[:, :n_groups], axis=-1)
    g_idx = jnp.argmax(grp_p, axis=-1, keepdims=True)
    g_prob = jnp.max(grp_p, axis=-1, keepdims=True)
    e_logit = logits[:, n_groups:n_groups + n_exp].reshape(t, n_groups, per_group)
    in_grp = jnp.take_along_axis(e_logit, g_idx[:, :, None], axis=1)[:, 0, :]
    i1 = jnp.argmax(in_grp, axis=-1, keepdims=True)
    rest = jnp.where(jnp.arange(per_group)[None, :] == i1, -jnp.inf, in_grp)
    i2 = jnp.argmax(rest, axis=-1, keepdims=True)
    top_i = jnp.concatenate([i1, i2], axis=-1)
    top_v = jnp.concatenate([jnp.max(in_grp, axis=-1, keepdims=True), jnp.max(rest, axis=-1, keepdims=True)], axis=-1)
    gate = g_prob * jax.nn.softmax(top_v, axis=-1)
    expert = (g_idx * per_group + top_i).reshape(-1).astype(jnp.int32)
    a = t * top_k
    onehot = expert[:, None] == jnp.arange(n_exp, dtype=jnp.int32)[None, :]
    csum = jnp.cumsum(onehot.astype(jnp.int32), axis=0)
    counts = csum[-1]
    rank = jnp.sum(jnp.where(onehot, csum - 1, 0), axis=1)
    padded = (counts + blk - 1) // blk * blk
    pad_end = jnp.cumsum(padded)
    pad_start = pad_end - padded
    dest = (pad_start[expert] + rank).astype(jnp.int32)
    n_blk = (a + n_exp * (blk - 1) + blk - 1) // blk
    tok = jnp.arange(a, dtype=jnp.int32) // top_k
    row_tok = jnp.zeros((n_blk * blk,), jnp.int32).at[dest].set(tok)
    blk_start = jnp.arange(n_blk, dtype=jnp.int32) * blk
    blk_expert = jnp.minimum(jnp.sum(pad_end[None, :] <= blk_start[:, None], axis=1), n_exp - 1).astype(jnp.int32)
    n_used = (pad_end[-1:] // blk).astype(jnp.int32)
    return gate, dest.reshape(t, top_k), row_tok, blk_expert, n_used, n_blk


def _pick(n, *prefs):
    for p in prefs:
        if n % p == 0:
            return p
    return n


def kernel(x_prompt, x_sample, mem_prompt, cache_ckv, cache_krope, state_ret, cache_mem_k, cache_mem_v, page_table, attn_norm_w, w_in, mla_q_norm_w, mla_w_uq, mla_kv_norm_w, mla_w_uk, mla_w_uv, ret_gn_w, mem_norm_w, mem_w_kv, w_branch_a, w_branch_b, w_branch_c, w_out, ffn_norm_w, router_grp_w, router_grp_b, router_exp_w, router_exp_b, exp_w_gate, exp_w_up, exp_w_down, final_norm_w):
    batch, seq, d = x_prompt.shape
    dec_batch, dec_seq, _ = x_sample.shape
    depth = w_in.shape[0]
    q_rank, n_heads, qh = mla_w_uq.shape[1:]
    kv_rank, _, nope = mla_w_uk.shape[1:]
    rope_dim = qh - nope
    v_dim = mla_w_uv.shape[3]
    ret_heads, ret_dv = ret_gn_w.shape[1:]
    ret_dk = state_ret.shape[3]
    mem_tokens = mem_prompt.shape[1]
    mem_heads, mem_hd = cache_mem_k.shape[3:]
    mem_width = mem_heads * mem_hd
    n_groups = router_grp_w.shape[2]
    n_exp = router_exp_w.shape[2]
    per_group = n_exp // n_groups
    top_k = 2
    page = cache_ckv.shape[2]
    past_len = page_table.shape[1] * page
    assert nope == LANES and v_dim == LANES and rope_dim == 64 and ret_dk == LANES
    assert q_rank == kv_rank and n_groups + n_exp <= LANES

    tp = batch * seq
    ts = dec_batch * dec_seq
    t = tp + ts
    ret_qk = ret_heads * ret_dk
    ret_v = ret_heads * ret_dv
    cq_off, ckv_off = 0, q_rank
    rq_off = ckv_off + kv_rank
    rk_off = rq_off + ret_qk
    rv_off = rk_off + ret_qk
    rg_off = rv_off + ret_v
    mq_off = rg_off + ret_v
    gz_off = mq_off + mem_width
    ret_offs = (rq_off, rk_off, rv_off, rg_off)

    pos = jnp.concatenate([jnp.tile(jnp.arange(seq), batch), jnp.tile(past_len + jnp.arange(dec_seq), dec_batch)])
    cos64, sin64 = _rope_tables(pos, rope_dim)
    cos128, sin128 = _rope_tables(pos, ret_dk)
    lgam = jnp.log1p(-jnp.exp2(-5.0 - jnp.arange(ret_heads, dtype=F32)))
    lgam = jnp.broadcast_to(lgam[:, None, None], (ret_heads, 1, LANES))

    xp = x_prompt.reshape(tp, d)
    xs = x_sample.reshape(ts, d)
    x = jnp.concatenate([xp, xs], axis=0)

    assert depth == 1, "the fused final norm assumes a single layer"
    l = 0
    tm_mid = _pick(t, 256)

    wi = w_in[l]
    kr_lo = q_rank + kv_rank
    w_main = jnp.concatenate([wi[:, :kr_lo], wi[:, kr_lo + rope_dim:]], axis=1).astype(BF16)
    w_kr = jnp.pad(wi[:, kr_lo:kr_lo + rope_dim], ((0, 0), (0, LANES - rope_dim))).astype(BF16)
    z, zk = _norm_proj(x, attn_norm_w[l], w_main, w_kr, tm=_pick(t, 1536, 1024, 512),
                       tn=_pick(w_main.shape[1], 512))

    wuq = mla_w_uq[l]
    wqn = wuq[:, :, :nope].reshape(q_rank, n_heads * LANES).astype(BF16)
    wqr = jnp.pad(wuq[:, :, nope:], ((0, 0), (0, 0), (0, LANES - rope_dim))).reshape(q_rank, n_heads * LANES).astype(BF16)
    wuk = mla_w_uk[l].reshape(kv_rank, n_heads * nope).astype(BF16)
    wuv = mla_w_uv[l].reshape(kv_rank, n_heads * v_dim).astype(BF16)
    qcat, c, kr, kcat, v = _mla_prep(
        z, zk, cos64, sin64, mla_q_norm_w[l].reshape(1, -1), mla_kv_norm_w[l].reshape(1, -1), wqn, wqr, wuk, wuv,
        tm=tm_mid, n_heads=n_heads, q_rank=q_rank, kv_rank=kv_rank, rope_dim=rope_dim)

    scale = qh ** -0.5
    oa_p = _mla_prompt(qcat, kcat, v, batch=batch, seq=seq, n_heads=n_heads, tq=_pick(seq, 512), scale=scale,
                       total_rows=tp)
    wukt = jnp.transpose(mla_w_uk[l], (1, 2, 0)).astype(BF16)
    qlat, qr = _absorb_q(qcat, wukt, ts=ts, row_block=tp // ts, n_heads=n_heads, kv_rank=kv_rank)
    rows = dec_seq * n_heads
    new_pad = LANES - dec_seq
    cnew = jnp.pad(c[tp:].reshape(dec_batch, dec_seq, kv_rank), ((0, 0), (0, new_pad), (0, 0))).astype(BF16)
    krnew = jnp.pad(kr[tp:].reshape(dec_batch, dec_seq, rope_dim), ((0, 0), (0, new_pad), (0, 0))).astype(BF16)
    o_lat = _mla_decode(
        page_table, qlat.reshape(dec_batch, rows, kv_rank), qr.reshape(dec_batch, rows, LANES), cnew, krnew,
        cache_ckv[l], jnp.swapaxes(cache_krope[l], 1, 2), npg=_pick(page_table.shape[1], 32), scale=scale,
        n_heads=n_heads, rope_dim=rope_dim)
    wuv_h = jnp.transpose(mla_w_uv[l], (1, 0, 2)).astype(BF16)
    oa_s = _head_proj(o_lat.reshape(ts, n_heads * kv_rank), wuv_h, n_heads=n_heads)

    gn_w = ret_gn_w[l].reshape(1, ret_v)
    ob_p, st_p = _ret_prompt(z, cos128, sin128, lgam, gn_w, batch=batch, seq=seq, n_heads=ret_heads, dk=ret_dk,
                             dv=ret_dv, rows=_pick(seq, 1024, 512), offs=ret_offs, total_rows=tp)
    ob_s, st_s = _ret_sample(z, cos128, sin128, lgam, gn_w, state_ret[l], row0=tp, dec_batch=dec_batch,
                             dec_seq=dec_seq, n_heads=ret_heads, dk=ret_dk, dv=ret_dv, offs=ret_offs)

    mem_rows = batch * mem_tokens
    (kv,) = _norm_proj(mem_prompt.reshape(mem_rows, d), mem_norm_w[l], mem_w_kv[l].astype(BF16),
                       tm=_pick(mem_rows, 1024, 512), tn=_pick(2 * mem_width, 512))
    oc_p = _mem_attn_prompt(z, kv, batch=batch, seq=seq, mem_tokens=mem_tokens, n_heads=mem_heads, hd=mem_hd,
                            tq=_pick(seq, 512), mq_off=mq_off, total_rows=tp)
    mq_s = z[tp:, mq_off:mq_off + mem_width].reshape(dec_batch, dec_seq * mem_heads, mem_hd)
    oc_s = _mem_attn_sample(mq_s, cache_mem_k[l], cache_mem_v[l], nb=4).reshape(ts, mem_width)

    tm_merge = _pick(ts, 512, 256, 128)
    assert tp % tm_merge == 0
    merged = _merge((oa_p, ob_p, oc_p), (oa_s, ob_s, oc_s), w_branch_a[l].astype(BF16), w_branch_b[l].astype(BF16),
                    w_branch_c[l].astype(BF16), z, tm=tm_merge, tn=_pick(d, 1024), gz_off=gz_off)
    n_route = n_groups + n_exp
    w_router = jnp.pad(jnp.concatenate([router_grp_w[l], router_exp_w[l]], axis=1), ((0, 0), (0, LANES - n_route)))
    w_router_hi = w_router.astype(BF16)
    w_router_lo = (w_router - w_router_hi.astype(F32)).astype(BF16)
    b_router = jnp.pad(jnp.concatenate([router_grp_b[l], router_exp_b[l]]), (0, LANES - n_route)).reshape(1, LANES)
    x1, h2, logits = _out_proj(x, merged, w_out[l].astype(BF16), ffn_norm_w[l].reshape(1, d), w_router_hi,
                               w_router_lo, b_router, tm=tm_mid)

    gate, dest, row_tok, blk_expert, n_used, n_blk = _route(logits, n_groups, per_group, top_k, MOE_BLOCK)
    y_rows = _experts(blk_expert, n_used, row_tok, h2, exp_w_gate[l], exp_w_up[l], exp_w_down[l], n_blk=n_blk,
                      blk=MOE_BLOCK)
    g0 = jnp.broadcast_to(gate[:, 0:1], (t, LANES))
    g1 = jnp.broadcast_to(gate[:, 1:2], (t, LANES))
    tm_c = _pick(ts, 128)
    assert tp % tm_c == 0
    y_p, y_s = _combine(dest[:, 0], dest[:, 1], y_rows, x1, g0, g1, final_norm_w.reshape(1, d), tm=tm_c, rows_first=tp)

    lead = lambda a: a[None]
    return (
        y_p.reshape(batch, seq, d),
        y_s.reshape(dec_batch, dec_seq, d),
        lead(c[:tp].reshape(batch, seq, kv_rank)),
        lead(kr[:tp].reshape(batch, seq, rope_dim)),
        lead(st_p),
        lead(kv[:, :mem_width].reshape(batch, mem_tokens, mem_heads, mem_hd)),
        lead(kv[:, mem_width:].reshape(batch, mem_tokens, mem_heads, mem_hd)),
        lead(c[tp:].reshape(dec_batch, dec_seq, kv_rank)),
        lead(kr[tp:].reshape(dec_batch, dec_seq, rope_dim)),
        lead(st_s),
    )
```
